```python
import math
import jax, jax.numpy as jnp
from jax import lax
import numpy as np

D_MODEL = 1024
BATCH = 2
SEQ = 16384
DEPTH = 4

SB_HEAD_DIM = 64
SB_WIDTH = D_MODEL // 4
SB_HEADS = SB_WIDTH // SB_HEAD_DIM
SB_BLOCK = 128
DN_HEAD_DIM = 128
DN_WIDTH = D_MODEL // 2
DN_HEADS = DN_WIDTH // DN_HEAD_DIM
DN_CONV = 4
DN_CHUNK = 64
HG_VAL_DIM = 64
HG_KEY_DIM = 64
HG_WIDTH = D_MODEL // 4
HG_HEADS = HG_WIDTH // HG_VAL_DIM
HG_KEY_WIDTH = HG_HEADS * HG_KEY_DIM
HG_CHUNK = 16
D_MIX = SB_WIDTH + DN_WIDTH + HG_WIDTH
SPLIT_SIZES = (SB_WIDTH, SB_WIDTH, SB_WIDTH,
               3 * DN_WIDTH, DN_WIDTH, DN_HEADS, DN_HEADS,
               HG_KEY_WIDTH, HG_KEY_WIDTH, HG_WIDTH, HG_WIDTH)
D_IN_PROJ = sum(SPLIT_SIZES)
D_FF = -(-8 * D_MODEL // (3 * 256)) * 256
EPS = 1e-6

kernel_name = "hybrid_sb_gdn_hgrn2_trunk"


def rms_norm(x, gain):
    xf = x.astype(jnp.float32)
    y = xf * lax.rsqrt(jnp.mean(xf * xf, axis=-1, keepdims=True) + EPS)
    return (y * gain.astype(jnp.float32)).astype(x.dtype)


def l2_norm(x):
    xf = x.astype(jnp.float32)
    return xf * lax.rsqrt(jnp.sum(xf * xf, axis=-1, keepdims=True) + EPS)


def split_heads(x, n_heads):
    b, s, _ = x.shape
    return x.reshape(b, s, n_heads, -1)


def causal_depthwise_conv(x, w):
    k_size, ch = w.shape
    return lax.conv_general_dilated(
        x, w[:, None, :].astype(x.dtype), window_strides=(1,),
        padding=[(k_size - 1, 0)], dimension_numbers=("NWC", "WIO", "NWC"),
        feature_group_count=ch)


def stick_breaking_attention(q, k, v):
    b, s, h, d = q.shape
    nb = s // SB_BLOCK
    scale = d ** -0.5
    qf = q.astype(jnp.float32)
    kf = k.astype(jnp.float32)
    vf = v.astype(jnp.float32)
    ar = jnp.arange(SB_BLOCK)
    rev_incl = (ar[:, None] >= ar[None, :]).astype(jnp.float32)
    outs = []
    for i in range(nb):
        n_keys = (i + 1) * SB_BLOCK
        q_i = qf[:, i * SB_BLOCK:(i + 1) * SB_BLOCK]
        k_p = kf[:, :n_keys]
        v_p = vf[:, :n_keys]
        z = jnp.einsum("bqhd,bkhd->bhqk", q_i, k_p) * scale
        past = jnp.arange(n_keys)[None, :] < (i * SB_BLOCK + ar)[:, None]
        log_keep = jnp.where(past, jax.nn.log_sigmoid(-z), 0.0)
        within = jnp.einsum("bhqnk,kj->bhqnj",
                            log_keep.reshape(b, h, SB_BLOCK, i + 1, SB_BLOCK), rev_incl)
        nblk = jnp.arange(i + 1)
        rev_excl = (nblk[:, None] > nblk[None, :]).astype(jnp.float32)
        offset = jnp.einsum("bhqn,nm->bhqm", within[..., 0], rev_excl)
        suffix = (within + offset[..., None]).reshape(b, h, SB_BLOCK, n_keys)
        w = jnp.exp(jnp.where(past, z + suffix, -jnp.inf))
        outs.append(jnp.einsum("bhqk,bkhd->bqhd", w, v_p))
    return jnp.concatenate(outs, axis=1).astype(q.dtype)


def _chunkify(x, c):
    b, s = x.shape[:2]
    return x.reshape(b, s // c, c, *x.shape[2:]).swapaxes(2, 3)


def gated_delta_rule(q, k, v, g, beta):
    b, s, h, dk = q.shape
    dv = v.shape[-1]
    c = DN_CHUNK
    q = _chunkify(q.astype(jnp.float32) * dk ** -0.5, c)
    k = _chunkify(k.astype(jnp.float32), c)
    v = _chunkify(v.astype(jnp.float32), c)
    g = _chunkify(g.astype(jnp.float32), c)
    beta = _chunkify(beta.astype(jnp.float32), c)
    causal = jnp.tril(jnp.ones((c, c), dtype=bool))
    strict = jnp.tril(jnp.ones((c, c), dtype=bool), k=-1)
    gc = jnp.einsum("bnhj,ij->bnhi", g, causal.astype(jnp.float32))
    diff = gc[..., :, None] - gc[..., None, :]
    decay = jnp.where(causal, jnp.exp(jnp.where(causal, diff, 0.0)), 0.0)
    k_beta = k * beta[..., None]
    v_beta = v * beta[..., None]
    m = jnp.where(strict, jnp.einsum("bnhid,bnhjd->bnhij", k_beta, k) * decay, 0.0)
    rhs = jnp.concatenate([v_beta, k_beta * jnp.exp(gc)[..., None]], axis=-1)
    sol = lax.linalg.triangular_solve(m, rhs, left_side=True, lower=True, unit_diagonal=True)
    u, w = sol[..., :dv], sol[..., dv:]
    attn_intra = jnp.einsum("bnhid,bnhjd->bnhij", q, k) * decay
    q_dec = q * jnp.exp(gc)[..., None]
    k_dec = k * jnp.exp(gc[..., -1:] - gc)[..., None]
    chunk_decay = jnp.exp(gc[..., -1])

    def step(state, xs):
        a_i, q_i, k_i, u_i, w_i, cd_i = xs
        v_new = u_i - jnp.einsum("bhck,bhkv->bhcv", w_i, state)
        o = jnp.einsum("bhck,bhkv->bhcv", q_i, state) + jnp.einsum("bhij,bhjv->bhiv", a_i, v_new)
        state = state * cd_i[..., None, None] + jnp.einsum("bhck,bhcv->bhkv", k_i, v_new)
        return state, o

    xs = tuple(jnp.moveaxis(t, 1, 0) for t in (attn_intra, q_dec, k_dec, u, w, chunk_decay))
    s0 = jnp.zeros((b, h, dk, dv), jnp.float32)
    _, o = lax.scan(step, s0, xs)
    return o.transpose(1, 0, 3, 2, 4).reshape(b, s, h, dv)


def hgrn2_recurrence(q, k, v, log_f):
    b, s, h, dk = q.shape
    dv = v.shape[-1]
    c = HG_CHUNK
    causal2 = jnp.tril(jnp.ones((c, c), dtype=bool))
    causal = causal2[..., None]
    tri = causal2.astype(jnp.float32)
    xs = tuple(jnp.moveaxis(_chunkify(t.astype(jnp.float32), c), 1, 0) for t in (q, k, v, log_f))

    def step(state, xs_i):
        q_i, k_i, v_i, lf_i = xs_i
        bcum = jnp.einsum("bhsk,ts->bhtk", lf_i, tri)
        o_inter = jnp.einsum("bhck,bhkv->bhcv", q_i * jnp.exp(bcum), state)
        diff = bcum[:, :, :, None, :] - bcum[:, :, None, :, :]
        dec = jnp.where(causal, jnp.exp(jnp.where(causal, diff, 0.0)), 0.0)
        scores = jnp.einsum("bhtsk,bhsk->bhts", dec * q_i[:, :, :, None, :], k_i)
        o = o_inter + jnp.einsum("bhts,bhsv->bhtv", scores, v_i)
        last = bcum[:, :, -1:, :]
        state = jnp.exp(last[:, :, 0, :])[..., None] * state + jnp.einsum(
            "bhsk,bhsv->bhkv", k_i * jnp.exp(last - bcum), v_i)
        return state, o

    s0 = jnp.zeros((b, h, dk, dv), jnp.float32)
    _, o = lax.scan(step, s0, xs)
    return o.transpose(1, 0, 3, 2, 4).reshape(b, s, h, dv)


def hybrid_layer(x, norm1, w_in, sb_gain, dn_conv, dn_a_log, dn_dt_bias, dn_gain,
                 hg_lb, hg_gain, w_out, norm2, w_gate, w_up, w_down):
    b, s, _ = x.shape
    h = rms_norm(x, norm1)
    proj = h @ w_in.astype(h.dtype)
    idx, acc = [], 0
    for n in SPLIT_SIZES[:-1]:
        acc += n
        idx.append(acc)
    (sb_q, sb_k, sb_v, dn_qkv, dn_z, dn_a, dn_b,
     hg_q, hg_f, hg_i, hg_g) = jnp.split(proj, idx, axis=-1)

    sb_o = stick_breaking_attention(split_heads(sb_q, SB_HEADS), split_heads(sb_k, SB_HEADS),
                                    split_heads(sb_v, SB_HEADS))
    sb_o = rms_norm(sb_o, sb_gain)

    dn_qkv = jax.nn.silu(causal_depthwise_conv(dn_qkv, dn_conv))
    dn_q, dn_k, dn_v = jnp.split(dn_qkv, 3, axis=-1)
    dn_g = -jnp.exp(dn_a_log.astype(jnp.float32)) * jax.nn.softplus(
        dn_a.astype(jnp.float32) + dn_dt_bias.astype(jnp.float32))
    dn_beta = jax.nn.sigmoid(dn_b.astype(jnp.float32))
    dn_o = gated_delta_rule(l2_norm(split_heads(dn_q, DN_HEADS)), l2_norm(split_heads(dn_k, DN_HEADS)),
                            split_heads(dn_v, DN_HEADS), dn_g, dn_beta).astype(x.dtype)
    dn_o = rms_norm(dn_o, dn_gain) * jax.nn.silu(split_heads(dn_z, DN_HEADS))

    lb = hg_lb.astype(jnp.float32)
    hf = hg_f.astype(jnp.float32)
    log_f = jnp.logaddexp(jnp.log(lb), jnp.log1p(-lb) + jax.nn.log_sigmoid(hf))
    hg_k = (1.0 - lb) * jax.nn.sigmoid(-hf)
    hg_o = hgrn2_recurrence(split_heads(jax.nn.silu(hg_q), HG_HEADS), split_heads(hg_k, HG_HEADS),
                            split_heads(hg_i, HG_HEADS), split_heads(log_f, HG_HEADS)).astype(x.dtype)
    hg_o = rms_norm(hg_o, hg_gain) * jax.nn.sigmoid(split_heads(hg_g, HG_HEADS))

    mix = jnp.concatenate([sb_o.reshape(b, s, SB_WIDTH), dn_o.reshape(b, s, DN_WIDTH),
                           hg_o.reshape(b, s, HG_WIDTH)], axis=-1)
    x = x + mix @ w_out.astype(mix.dtype)

    h2 = rms_norm(x, norm2)
    ff = jax.nn.silu(h2 @ w_gate.astype(h2.dtype)) * (h2 @ w_up.astype(h2.dtype))
    return x + ff @ w_down.astype(ff.dtype)


def setup_inputs(seed: int = 0) -> dict:
    key = jax.random.key(seed)
    ks = jax.random.split(key, 20)
    f32 = jnp.float32

    def gain(k, shape):
        return 1.0 + 0.02 * jax.random.normal(k, shape, f32)

    dt = jnp.exp(jax.random.uniform(ks[6], (DEPTH, DN_HEADS), f32, math.log(1e-3), math.log(1e-1)))
    return {
        "x": jax.random.normal(ks[0], (BATCH, SEQ, D_MODEL), f32),
        "norm1": gain(ks[1], (DEPTH, D_MODEL)),
        "w_in": jax.random.normal(ks[2], (DEPTH, D_MODEL, D_IN_PROJ), f32) * D_MODEL ** -0.5,
        "sb_gain": gain(ks[3], (DEPTH, SB_HEAD_DIM)),
        "dn_conv": jax.random.normal(ks[4], (DEPTH, DN_CONV, 3 * DN_WIDTH), f32) * DN_CONV ** -0.5,
        "dn_a_log": jnp.log(jax.random.uniform(ks[5], (DEPTH, DN_HEADS), f32, 1.0, 16.0)),
        "dn_dt_bias": dt + jnp.log(-jnp.expm1(-dt)),
        "dn_gain": gain(ks[7], (DEPTH, DN_HEAD_DIM)),
        "hg_lb_logits": 0.1 * jax.random.normal(ks[8], (DEPTH, HG_KEY_WIDTH), f32),
        "hg_gain": gain(ks[9], (DEPTH, HG_VAL_DIM)),
        "w_out": jax.random.normal(ks[10], (DEPTH, D_MIX, D_MODEL), f32) * D_MIX ** -0.5,
        "norm2": gain(ks[11], (DEPTH, D_MODEL)),
        "w_gate": jax.random.normal(ks[12], (DEPTH, D_MODEL, D_FF), f32) * D_MODEL ** -0.5,
        "w_up": jax.random.normal(ks[13], (DEPTH, D_MODEL, D_FF), f32) * D_MODEL ** -0.5,
        "w_down": jax.random.normal(ks[14], (DEPTH, D_FF, D_MODEL), f32) * D_FF ** -0.5,
        "final_norm": gain(ks[15], (D_MODEL,)),
    }


def reference(x, norm1, w_in, sb_gain, dn_conv, dn_a_log, dn_dt_bias, dn_gain, hg_lb_logits,
              hg_gain, w_out, norm2, w_gate, w_up, w_down, final_norm):
    lbs = jnp.cumsum(jax.nn.softmax(hg_lb_logits.astype(jnp.float32), axis=0), axis=0)
    lbs = lbs - lbs[0:1]
    for l in range(DEPTH):
        x = hybrid_layer(x, norm1[l], w_in[l], sb_gain[l], dn_conv[l], dn_a_log[l], dn_dt_bias[l],
                         dn_gain[l], lbs[l], hg_gain[l], w_out[l], norm2[l], w_gate[l], w_up[l],
                         w_down[l])
    return rms_norm(x, final_norm)
```

```python
import functools

import jax
import jax.numpy as jnp
from jax import lax
from jax.experimental import pallas as pl
from jax.experimental.pallas import tpu as pltpu

EPS = 1e-6
F32 = jnp.float32
BF16 = jnp.bfloat16

D_MODEL = 1024
SB_HEAD_DIM = 64
SB_WIDTH = 256
DN_HEAD_DIM = 128
DN_HEADS = 4
DN_WIDTH = 512
DN_CONV = 4
DN_CHUNK = 64
DN_SUB = 16
HG_HEADS = 4
HG_DIM = 64
HG_WIDTH = 256
HG_CHUNK = 16
LANES = 128
SUBLANES = 8
VMEM_LIMIT_BYTES = 56 * 1024 * 1024


def _params(*sem):
    return pltpu.CompilerParams(dimension_semantics=sem, vmem_limit_bytes=VMEM_LIMIT_BYTES)


def _split3(a):
    hi = a.astype(BF16)
    lo = (a - hi.astype(F32)).astype(BF16)
    return hi, lo


def _dot(a, b):
    return jnp.dot(a, b, preferred_element_type=F32)


def _dot3(a, b):
    ah, al = _split3(a)
    bh, bl = _split3(b)
    return _dot(ah, bh) + (_dot(ah, bl) + _dot(al, bh))


def _dot_exact_rhs(a, b_bf16):
    ah, al = _split3(a)
    return _dot(ah, b_bf16) + _dot(al, b_bf16)


def _softplus(x):
    return jnp.maximum(x, 0.0) + jnp.log(1.0 + jnp.exp(-jnp.abs(x)))


def _sigmoid(x):
    return 1.0 / (1.0 + jnp.exp(-x))


def _norm_matmul_kernel(x_ref, g_ref, w_ref, o_ref, y_ref):
    @pl.when(pl.program_id(1) == 0)
    def _():
        x = x_ref[...]
        y = x * lax.rsqrt(jnp.mean(x * x, axis=-1, keepdims=True) + EPS)
        y_ref[...] = (y * g_ref[...]).astype(BF16)

    o_ref[...] = _dot(y_ref[...], w_ref[...]).astype(o_ref.dtype)


def _norm_matmul(x, gain, w, out_dtype, tm, tn):
    m, d = x.shape
    n = w.shape[1]
    return pl.pallas_call(
        _norm_matmul_kernel,
        grid=(m // tm, n // tn),
        in_specs=[
            pl.BlockSpec((tm, d), lambda i, j: (i, 0)),
            pl.BlockSpec((1, d), lambda i, j: (0, 0)),
            pl.BlockSpec((d, tn), lambda i, j: (0, j)),
        ],
        out_specs=pl.BlockSpec((tm, tn), lambda i, j: (i, j)),
        out_shape=jax.ShapeDtypeStruct((m, n), out_dtype),
        scratch_shapes=[pltpu.VMEM((tm, d), BF16)],
        compiler_params=_params("parallel", "arbitrary"),
        name="norm_matmul",
    )(x, gain.reshape(1, d), w)


def _sb_attn_kernel(q_ref, k_ref, v_ref, gain_ref, o_ref, acc_ref, off_ref, *, blk):
    qi = pl.program_id(2)
    q = q_ref[0] * jnp.asarray(SB_HEAD_DIM ** -0.5, BF16)
    head0 = lax.broadcasted_iota(jnp.int32, (blk, LANES), 1) < SB_HEAD_DIM
    row = lax.broadcasted_iota(jnp.int32, (2 * blk, 2 * blk), 0)
    col = lax.broadcasted_iota(jnp.int32, (2 * blk, 2 * blk), 1)
    rev = jnp.where((row >= col) & ((row >= blk) == (col >= blk)), 1.0, 0.0).astype(BF16)
    qrow = lax.broadcasted_iota(jnp.int32, (blk, 2 * blk), 0)
    kcol = lax.broadcasted_iota(jnp.int32, (blk, 2 * blk), 1)
    kcol = jnp.where(kcol >= blk, kcol - blk, kcol)
    past = kcol < qrow

    acc_ref[...] = jnp.zeros_like(acc_ref)
    off_ref[...] = jnp.zeros_like(off_ref)

    def tile(kj, diag):
        ks = pl.multiple_of(kj * blk, blk)
        k = k_ref[0, pl.ds(ks, blk), :]
        v = v_ref[0, pl.ds(ks, blk), :]
        zero = jnp.zeros_like(k)
        kcat = jnp.concatenate([jnp.where(head0, k, zero), jnp.where(head0, zero, k)], axis=0)
        vcat = jnp.concatenate([jnp.where(head0, v, zero), jnp.where(head0, zero, v)], axis=0)
        z = lax.dot_general(q, kcat, (((1,), (1,)), ((), ())), preferred_element_type=F32)
        sp = _softplus(z)
        if diag:
            sp = jnp.where(past, sp, 0.0)
        cum = _dot(sp.astype(BF16), rev)
        off = off_ref[...]
        w = jnp.exp(z - cum - off)
        if diag:
            w = jnp.where(past, w, 0.0)
        acc_ref[...] += _dot(w.astype(BF16), vcat)
        tot0 = jnp.sum(sp[:, :blk], axis=1, keepdims=True)
        tot1 = jnp.sum(sp[:, blk:], axis=1, keepdims=True)
        off_ref[...] = off + jnp.concatenate(
            [jnp.broadcast_to(tot0, (blk, blk)), jnp.broadcast_to(tot1, (blk, blk))], axis=1)

    tile(qi, True)

    def body(it, carry):
        tile(qi - 1 - it, False)
        return carry

    lax.fori_loop(0, qi, body, 0)

    o = acc_ref[...]
    oo = o * o
    ms0 = jnp.sum(jnp.where(head0, oo, 0.0), axis=1, keepdims=True)
    ms1 = jnp.sum(jnp.where(head0, 0.0, oo), axis=1, keepdims=True)
    ms = jnp.where(head0, ms0, ms1) * (1.0 / SB_HEAD_DIM)
    o_ref[0] = (o * lax.rsqrt(ms + EPS) * gain_ref[...]).astype(o_ref.dtype)


def _sb_attention(sb, gain, blk):
    b, s, _ = sb.shape
    npair = SB_WIDTH // LANES
    gain2 = jnp.tile(gain.astype(F32), LANES // SB_HEAD_DIM).reshape(1, LANES)
    return pl.pallas_call(
        functools.partial(_sb_attn_kernel, blk=blk),
        grid=(b, npair, s // blk),
        in_specs=[
            pl.BlockSpec((1, blk, LANES), lambda bi, p, qi: (bi, qi, p)),
            pl.BlockSpec((1, s, LANES), lambda bi, p, qi: (bi, 0, npair + p)),
            pl.BlockSpec((1, s, LANES), lambda bi, p, qi: (bi, 0, 2 * npair + p)),
            pl.BlockSpec((1, LANES), lambda bi, p, qi: (0, 0)),
        ],
        out_specs=pl.BlockSpec((1, blk, LANES), lambda bi, p, qi: (bi, qi, p)),
        out_shape=jax.ShapeDtypeStruct((b, s, SB_WIDTH), BF16),
        scratch_shapes=[pltpu.VMEM((blk, LANES), F32), pltpu.VMEM((blk, 2 * blk), F32)],
        compiler_params=_params("parallel", "parallel", "arbitrary"),
        name="sb_attention",
    )(sb, sb, sb, gain2)


def _unit_lower_inverse(m):
    c = DN_CHUNK
    row = lax.broadcasted_iota(jnp.int32, (c, c), 0)
    col = lax.broadcasted_iota(jnp.int32, (c, c), 1)
    eye = jnp.where(row == col, 1.0, 0.0).astype(F32)
    same = (row // DN_SUB) == (col // DN_SUB)
    d = jnp.where(same, m, 0.0)
    low = m - d
    p = eye - d
    dk = d
    n_sq = DN_SUB.bit_length() - 2
    for _ in range(n_sq):
        dk = _dot3(dk, dk)
        p = p + _dot3(p, dk)
    n = _dot3(p, low)
    t = eye - n
    nk = n
    for _ in range((c // DN_SUB).bit_length() - 2):
        nk = _dot3(nk, nk)
        t = t + _dot3(t, nk)
    return _dot3(t, p)


def _gdn_kernel(qkv_ref, z_ref, ab_ref, cw_ref, alog_ref, dtb_ref, gain_ref, o_ref,
                xpad_ref, y_ref, state_ref, *, nc):
    c = DN_CHUNK
    tb = nc * c
    pad = SUBLANES

    @pl.when(pl.program_id(1) == 0)
    def _():
        state_ref[...] = jnp.zeros_like(state_ref)
        xpad_ref[0:pad, :] = jnp.zeros((pad, xpad_ref.shape[1]), F32)

    xpad_ref[pad:pad + tb, :] = qkv_ref[0]
    conv = xpad_ref[pl.ds(pad - DN_CONV + 1, tb), :] * cw_ref[0:1, :]
    for kk in range(1, DN_CONV):
        conv = conv + xpad_ref[pl.ds(pad - DN_CONV + 1 + kk, tb), :] * cw_ref[kk:kk + 1, :]
    xpad_ref[0:pad, :] = xpad_ref[tb:tb + pad, :]
    y_ref[...] = conv * _sigmoid(conv)

    row = lax.broadcasted_iota(jnp.int32, (c, c), 0)
    col = lax.broadcasted_iota(jnp.int32, (c, c), 1)
    causal = row >= col
    strict = row > col
    tril = jnp.where(causal, 1.0, 0.0).astype(BF16)
    alog = alog_ref[...]
    dtb = dtb_ref[...]
    gain = gain_ref[...]

    def chunk(ci, carry):
        r0 = pl.multiple_of(ci * c, c)
        ab = ab_ref[0, pl.ds(r0, c), :]
        g_all = -jnp.exp(alog) * _softplus(ab + dtb)
        beta_all = _sigmoid(ab)
        gc_all = _dot_exact_rhs_left(tril, g_all)
        gc_t = gc_all.T
        for h in range(DN_HEADS):
            lo = h * DN_HEAD_DIM
            q = y_ref[pl.ds(r0, c), lo:lo + DN_HEAD_DIM]
            k = y_ref[pl.ds(r0, c), DN_WIDTH + lo:DN_WIDTH + lo + DN_HEAD_DIM]
            v = y_ref[pl.ds(r0, c), 2 * DN_WIDTH + lo:2 * DN_WIDTH + lo + DN_HEAD_DIM]
            q = q * lax.rsqrt(jnp.sum(q * q, axis=-1, keepdims=True) + EPS) * (DN_HEAD_DIM ** -0.5)
            k = k * lax.rsqrt(jnp.sum(k * k, axis=-1, keepdims=True) + EPS)
            gcol = gc_all[:, h:h + 1]
            grow = gc_t[h:h + 1, :]
            beta = beta_all[:, DN_HEADS + h:DN_HEADS + h + 1]
            decay = jnp.where(causal, jnp.exp(jnp.where(causal, gcol - grow, 0.0)), 0.0)
            kb = k * beta
            vb = v * beta
            kbf = k.astype(BF16)
            kk_t = lax.dot_general(kb.astype(BF16), kbf, (((1,), (1,)), ((), ())),
                                   preferred_element_type=F32)
            m = jnp.where(strict, kk_t * decay, 0.0)
            t_inv = _unit_lower_inverse(m)
            egc = jnp.exp(gcol)
            u = _dot3(t_inv, vb)
            w = _dot3(t_inv, kb * egc)
            qk_t = lax.dot_general(q.astype(BF16), kbf, (((1,), (1,)), ((), ())),
                                   preferred_element_type=F32)
            attn = qk_t * decay
            glast = gc_all[c - 1:c, h:h + 1]
            q_dec = q * egc
            k_dec = k * jnp.exp(glast - gcol)
            state = state_ref[h]
            sbf = state.astype(BF16)
            v_new = u - _dot(w.astype(BF16), sbf)
            vnb = v_new.astype(BF16)
            o = _dot(q_dec.astype(BF16), sbf) + _dot(attn.astype(BF16), vnb)
            state_ref[h] = state * jnp.exp(glast) + lax.dot_general(
                k_dec.astype(BF16), vnb, (((0,), (0,)), ((), ())), preferred_element_type=F32)
            zz = z_ref[0, pl.ds(r0, c), lo:lo + DN_HEAD_DIM]
            on = o * lax.rsqrt(jnp.mean(o * o, axis=-1, keepdims=True) + EPS) * gain
            o_ref[0, pl.ds(r0, c), lo:lo + DN_HEAD_DIM] = (on * (zz * _sigmoid(zz))).astype(o_ref.dtype)
        return carry

    lax.fori_loop(0, nc, chunk, 0)


def _dot_exact_rhs_left(a_bf16, b):
    bh, bl = _split3(b)
    return _dot(a_bf16, bh) + _dot(a_bf16, bl)


def _gdn(rest, conv_w, a_log, dt_bias, gain, tb):
    b, s, _ = rest.shape
    nc = tb // DN_CHUNK
    wq = 3 * DN_WIDTH

    def lanes(vec, offset):
        return jnp.zeros((1, LANES), F32).at[0, offset:offset + vec.shape[0]].set(vec.astype(F32))

    return pl.pallas_call(
        functools.partial(_gdn_kernel, nc=nc),
        grid=(b, s // tb),
        in_specs=[
            pl.BlockSpec((1, tb, wq), lambda bi, t: (bi, t, 0)),
            pl.BlockSpec((1, tb, DN_WIDTH), lambda bi, t: (bi, t, wq // DN_WIDTH)),
            pl.BlockSpec((1, tb, LANES), lambda bi, t: (bi, t, _AB_COL // LANES)),
            pl.BlockSpec((DN_CONV, wq), lambda bi, t: (0, 0)),
            pl.BlockSpec((1, LANES), lambda bi, t: (0, 0)),
            pl.BlockSpec((1, LANES), lambda bi, t: (0, 0)),
            pl.BlockSpec((1, DN_HEAD_DIM), lambda bi, t: (0, 0)),
        ],
        out_specs=pl.BlockSpec((1, tb, DN_WIDTH), lambda bi, t: (bi, t, 0)),
        out_shape=jax.ShapeDtypeStruct((b, s, DN_WIDTH), BF16),
        scratch_shapes=[
            pltpu.VMEM((tb + 2 * SUBLANES, wq), F32),
            pltpu.VMEM((tb, wq), F32),
            pltpu.VMEM((DN_HEADS, DN_HEAD_DIM, DN_HEAD_DIM), F32),
        ],
        compiler_params=_params("parallel", "arbitrary"),
        name="gated_deltanet",
    )(rest, rest, rest, conv_w.astype(F32), lanes(a_log, 0), lanes(dt_bias, 0),
      gain.astype(F32).reshape(1, DN_HEAD_DIM))


def _hgrn2_kernel(q_ref, f_ref, i_ref, g_ref, lb_ref, gain_ref, o_ref, acc_ref, state_ref, *, nsub):
    c = HG_CHUNK
    w = HG_WIDTH

    @pl.when(pl.program_id(1) == 0)
    def _():
        state_ref[...] = jnp.zeros_like(state_ref)

    row = lax.broadcasted_iota(jnp.int32, (w, w), 0)
    col = lax.broadcasted_iota(jnp.int32, (w, w), 1)
    same_head = (row // HG_DIM) == (col // HG_DIM)
    head_ones = jnp.where(same_head, 1.0, 0.0).astype(BF16)
    r16 = lax.broadcasted_iota(jnp.int32, (c, c), 0)
    c16 = lax.broadcasted_iota(jnp.int32, (c, c), 1)
    tril = jnp.where(r16 >= c16, 1.0, 0.0).astype(BF16)
    trow = lax.broadcasted_iota(jnp.int32, (c, w), 0)
    lb = lb_ref[...]
    log_lb = jnp.log(lb)
    log_1mlb = jnp.log(1.0 - lb)

    def sub(si, carry):
        r0 = pl.multiple_of(si * c, c)
        hq = q_ref[0, pl.ds(r0, c), :]
        hf = f_ref[0, pl.ds(r0, c), :]
        v = i_ref[0, pl.ds(r0, c), :]
        q = hq * _sigmoid(hq)
        log_sig = jnp.minimum(hf, 0.0) - jnp.log(1.0 + jnp.exp(-jnp.abs(hf)))
        bterm = log_1mlb + log_sig
        mx = jnp.maximum(log_lb, bterm)
        lf = mx + jnp.log(jnp.exp(log_lb - mx) + jnp.exp(bterm - mx))
        k = (1.0 - lb) * _sigmoid(-hf)
        bcum = _dot_exact_rhs_left(tril, lf)
        last = bcum[c - 1:c, :]
        st = state_ref[...]
        o = lax.dot_general((q * jnp.exp(bcum)).astype(BF16), st.astype(BF16),
                            (((1,), (1,)), ((), ())), preferred_element_type=F32)
        xs = []
        for s_i in range(c):
            keep = trow >= s_i
            dec = jnp.where(keep, jnp.exp(jnp.where(keep, bcum - bcum[s_i:s_i + 1, :], 0.0)), 0.0)
            xs.append((dec * q * k[s_i:s_i + 1, :]).astype(BF16))
        scores = _dot(jnp.concatenate(xs, axis=0), head_ones)
        for s_i in range(c):
            o = o + scores[s_i * c:(s_i + 1) * c, :] * v[s_i:s_i + 1, :]
        acc_ref[pl.ds(r0, c), :] = o
        kd = k * jnp.exp(last - bcum)
        upd = lax.dot_general(v.astype(BF16), kd.astype(BF16), (((0,), (0,)), ((), ())),
                              preferred_element_type=F32)
        state_ref[...] = jnp.where(same_head, st * jnp.exp(last) + upd, 0.0)
        return carry

    lax.fori_loop(0, nsub, sub, 0)

    o = acc_ref[...]
    ms = _dot_exact_rhs(o * o, head_ones) * (1.0 / HG_DIM)
    gate = g_ref[0]
    o_ref[0] = (o * lax.rsqrt(ms + EPS) * gain_ref[...] * _sigmoid(gate)).astype(o_ref.dtype)


def _hgrn2(rest, lb, gain, tb):
    b, s, _ = rest.shape
    cb = _HG_COL // HG_WIDTH
    blk = lambda j: pl.BlockSpec((1, tb, HG_WIDTH), lambda bi, t: (bi, t, cb + j))
    vec = pl.BlockSpec((1, HG_WIDTH), lambda bi, t: (0, 0))
    return pl.pallas_call(
        functools.partial(_hgrn2_kernel, nsub=tb // HG_CHUNK),
        grid=(b, s // tb),
        in_specs=[blk(0), blk(1), blk(2), blk(3), vec, vec],
        out_specs=pl.BlockSpec((1, tb, HG_WIDTH), lambda bi, t: (bi, t, 0)),
        out_shape=jax.ShapeDtypeStruct((b, s, HG_WIDTH), BF16),
        scratch_shapes=[pltpu.VMEM((tb, HG_WIDTH), F32), pltpu.VMEM((HG_WIDTH, HG_WIDTH), F32)],
        compiler_params=_params("parallel", "arbitrary"),
        name="hgrn2",
    )(rest, rest, rest, rest, lb.astype(F32).reshape(1, HG_WIDTH),
      jnp.tile(gain.astype(F32), HG_HEADS).reshape(1, HG_WIDTH))


def _rms(x, gain):
    return x * lax.rsqrt(jnp.mean(x * x, axis=-1, keepdims=True) + EPS) * gain


def _out_ffn_kernel(x_ref, sb_ref, dn_ref, hg_ref, wsb_ref, wdn_ref, whg_ref, n2_ref,
                    wg_ref, wu_ref, wd_ref, fn_ref, o_ref, *, final):
    x1 = x_ref[...] + _dot(sb_ref[...], wsb_ref[...]) + _dot(dn_ref[...], wdn_ref[...]) \
        + _dot(hg_ref[...], whg_ref[...])
    h2 = _rms(x1, n2_ref[...]).astype(BF16)
    g = _dot(h2, wg_ref[...])
    u = _dot(h2, wu_ref[...])
    ff = (g * _sigmoid(g) * u).astype(BF16)
    out = x1 + _dot(ff, wd_ref[...])
    if final:
        out = _rms(out, fn_ref[...])
    o_ref[...] = out


def _out_ffn(x, sb_o, dn_o, hg_o, w_out, norm2, w_gate, w_up, w_down, final_norm, final, tm):
    m, d = x.shape
    dff = w_gate.shape[1]
    rows = lambda width: pl.BlockSpec((tm, width), lambda i: (i, 0))
    whole = lambda a: pl.BlockSpec(a.shape, lambda i: (0,) * a.ndim, pipeline_mode=pl.Buffered(1))
    wsb = w_out[:SB_WIDTH]
    wdn = w_out[SB_WIDTH:SB_WIDTH + DN_WIDTH]
    whg = w_out[SB_WIDTH + DN_WIDTH:]
    n2 = norm2.astype(F32).reshape(1, d)
    fn = final_norm.astype(F32).reshape(1, d)
    args = (x, sb_o, dn_o, hg_o, wsb, wdn, whg, n2, w_gate, w_up, w_down, fn)
    in_specs = [rows(d), rows(SB_WIDTH), rows(DN_WIDTH), rows(HG_WIDTH)] + [whole(a) for a in args[4:]]
    del dff
    return pl.pallas_call(
        functools.partial(_out_ffn_kernel, final=final),
        grid=(m // tm,),
        in_specs=in_specs,
        out_specs=rows(d),
        out_shape=jax.ShapeDtypeStruct((m, d), F32),
        compiler_params=_params("parallel"),
        name="out_ffn",
    )(*args)


_HG_COL = 3 * DN_WIDTH + DN_WIDTH
_AB_COL = _HG_COL + 4 * HG_WIDTH
_REST_WIDTH = _AB_COL + LANES


def _split_w_in(w_in):
    sb = w_in[:, :3 * SB_WIDTH]
    o = 3 * SB_WIDTH
    dn_qkvz = w_in[:, o:o + 4 * DN_WIDTH]
    o += 4 * DN_WIDTH
    dn_ab = w_in[:, o:o + 2 * DN_HEADS]
    o += 2 * DN_HEADS
    hg = w_in[:, o:]
    pad = jnp.zeros((w_in.shape[0], LANES - 2 * DN_HEADS), w_in.dtype)
    rest = jnp.concatenate([dn_qkvz, hg, dn_ab, pad], axis=1)
    return sb.astype(BF16), rest.astype(BF16)


def kernel(x, norm1, w_in, sb_gain, dn_conv, dn_a_log, dn_dt_bias, dn_gain, hg_lb_logits, hg_gain,
           w_out, norm2, w_gate, w_up, w_down, final_norm):
    b, s, d = x.shape
    depth = w_in.shape[0]
    lbs = jnp.cumsum(jax.nn.softmax(hg_lb_logits.astype(F32), axis=0), axis=0)
    lbs = lbs - lbs[0:1]
    xf = x.reshape(b * s, d)
    for l in range(depth):
        w_sb, w_rest = _split_w_in(w_in[l])
        sb = _norm_matmul(xf, norm1[l].astype(F32), w_sb, BF16, 512, 3 * SB_WIDTH)
        rest = _norm_matmul(xf, norm1[l].astype(F32), w_rest, F32, 512, 640)
        sb_o = _sb_attention(sb.reshape(b, s, -1), sb_gain[l], 128)
        rest3 = rest.reshape(b, s, -1)
        dn_o = _gdn(rest3, dn_conv[l], dn_a_log[l], dn_dt_bias[l], dn_gain[l], 512)
        hg_o = _hgrn2(rest3, lbs[l], hg_gain[l], 256)
        xf = _out_ffn(xf, sb_o.reshape(b * s, -1), dn_o.reshape(b * s, -1), hg_o.reshape(b * s, -1),
                      w_out[l].astype(BF16), norm2[l], w_gate[l].astype(BF16), w_up[l].astype(BF16),
                      w_down[l].astype(BF16), final_norm, l == depth - 1, 256)
    return xf.reshape(b, s, d)
```

```python
import functools

import jax
import jax.numpy as jnp
from jax import lax
from jax.experimental import pallas as pl
from jax.experimental.pallas import tpu as pltpu

EPS = 1e-6
F32 = jnp.float32
BF16 = jnp.bfloat16
LOG2E = 1.4426950408889634

D_MODEL = 1024
SB_HEAD_DIM = 64
SB_WIDTH = 256
SB_UNROLL = 4
DN_HEAD_DIM = 128
DN_HEADS = 4
DN_WIDTH = 512
DN_CONV = 4
DN_CHUNK = 64
DN_SUB = 16
DN_PREP_GROUP = 8
HG_HEADS = 4
HG_DIM = 64
HG_WIDTH = 256
HG_CHUNK = 16
LANES = 128
SUBLANES = 8
VMEM_LIMIT_BYTES = 56 * 1024 * 1024


def _params(*sem):
    return pltpu.CompilerParams(dimension_semantics=sem, vmem_limit_bytes=VMEM_LIMIT_BYTES)


def _split3(a):
    hi = a.astype(BF16)
    lo = (a - hi.astype(F32)).astype(BF16)
    return hi, lo


def _dot(a, b):
    return jnp.dot(a, b, preferred_element_type=F32)


def _dot_nt(a, b):
    return lax.dot_general(a, b, (((1,), (1,)), ((), ())), preferred_element_type=F32)


def _dot_tn(a, b):
    return lax.dot_general(a, b, (((0,), (0,)), ((), ())), preferred_element_type=F32)


def _dot_exact_rhs(a, b_bf16):
    ah, al = _split3(a)
    return _dot(ah, b_bf16) + _dot(al, b_bf16)


def _dot_exact_lhs(a_bf16, b):
    bh, bl = _split3(b)
    return _dot(a_bf16, bh) + _dot(a_bf16, bl)


def _softplus(x):
    neg_abs = lax.bitcast_convert_type(
        lax.bitcast_convert_type(x, jnp.uint32) | jnp.uint32(0x80000000), F32)
    return jnp.maximum(x, 0.0) + jnp.log(1.0 + jnp.exp2(neg_abs * LOG2E))


def _sigmoid(x):
    return 1.0 / (1.0 + jnp.exp(-x))


def _norm_matmul_kernel(x_ref, g_ref, w_ref, o_ref, y_ref):
    @pl.when(pl.program_id(1) == 0)
    def _():
        x = x_ref[...]
        y = x * lax.rsqrt(jnp.mean(x * x, axis=-1, keepdims=True) + EPS)
        y_ref[...] = (y * g_ref[...]).astype(BF16)

    o_ref[...] = _dot(y_ref[...], w_ref[...]).astype(o_ref.dtype)


def _norm_matmul(x, gain, w, out_dtype, tm, tn):
    m, d = x.shape
    n = w.shape[1]
    return pl.pallas_call(
        _norm_matmul_kernel,
        grid=(m // tm, n // tn),
        in_specs=[
            pl.BlockSpec((tm, d), lambda i, j: (i, 0)),
            pl.BlockSpec((1, d), lambda i, j: (0, 0)),
            pl.BlockSpec((d, tn), lambda i, j: (0, j)),
        ],
        out_specs=pl.BlockSpec((tm, tn), lambda i, j: (i, j)),
        out_shape=jax.ShapeDtypeStruct((m, n), out_dtype),
        scratch_shapes=[pltpu.VMEM((tm, d), BF16)],
        compiler_params=_params("parallel", "arbitrary"),
        name="norm_matmul",
    )(x, gain.reshape(1, d), w)


def _sb_attn_kernel(q_ref, k_ref, v_ref, gain_ref, o_ref, acc_ref, off0_ref, off1_ref, *, tq):
    tk = LANES
    ndiag = tq // tk
    qi = pl.program_id(2)
    q = q_ref[0] * jnp.asarray(SB_HEAD_DIM ** -0.5, BF16)
    head0 = lax.broadcasted_iota(jnp.int32, (tk, LANES), 1) < SB_HEAD_DIM
    row = lax.broadcasted_iota(jnp.int32, (2 * tk, 2 * tk), 0)
    col = lax.broadcasted_iota(jnp.int32, (2 * tk, 2 * tk), 1)
    rev = jnp.where((row >= col) & ((row >= tk) == (col >= tk)), 1.0, 0.0).astype(BF16)
    qrow = lax.broadcasted_iota(jnp.int32, (tq, 2 * tk), 0)
    kcol = lax.broadcasted_iota(jnp.int32, (tq, 2 * tk), 1)
    kcol = jnp.where(kcol >= tk, kcol - tk, kcol)

    acc_ref[...] = jnp.zeros_like(acc_ref)
    off0_ref[...] = jnp.zeros_like(off0_ref)
    off1_ref[...] = jnp.zeros_like(off1_ref)

    def tile(kj, past):
        ks = pl.multiple_of(kj * tk, tk)
        k = k_ref[0, pl.ds(ks, tk), :]
        v = v_ref[0, pl.ds(ks, tk), :]
        zero = jnp.zeros_like(k)
        kcat = jnp.concatenate([jnp.where(head0, k, zero), jnp.where(head0, zero, k)], axis=0)
        vcat = jnp.concatenate([jnp.where(head0, v, zero), jnp.where(head0, zero, v)], axis=0)
        z = _dot_nt(q, kcat)
        sp = _softplus(z)
        if past is not None:
            sp = jnp.where(past, sp, 0.0)
        cum = _dot(sp.astype(BF16), rev)
        off0 = off0_ref[...]
        off1 = off1_ref[...]
        w = jnp.exp(z - cum - jnp.concatenate([off0, off1], axis=1))
        if past is not None:
            w = jnp.where(past, w, 0.0)
        acc_ref[...] += _dot(w.astype(BF16), vcat)
        off0_ref[...] = off0 + jnp.sum(sp[:, :tk], axis=1, keepdims=True)
        off1_ref[...] = off1 + jnp.sum(sp[:, tk:], axis=1, keepdims=True)

    for d in range(ndiag - 1, -1, -1):
        tile(qi * ndiag + d, kcol + d * tk < qrow)

    def body(it, carry):
        kj = qi * ndiag - 1 - SB_UNROLL * it
        for u in range(SB_UNROLL):
            tile(kj - u, None)
        return carry

    lax.fori_loop(0, qi * (ndiag // SB_UNROLL), body, 0)

    head0q = lax.broadcasted_iota(jnp.int32, (tq, LANES), 1) < SB_HEAD_DIM
    o = acc_ref[...]
    oo = o * o
    ms0 = jnp.sum(jnp.where(head0q, oo, 0.0), axis=1, keepdims=True)
    ms1 = jnp.sum(jnp.where(head0q, 0.0, oo), axis=1, keepdims=True)
    ms = jnp.where(head0q, ms0, ms1) * (1.0 / SB_HEAD_DIM)
    o_ref[0] = (o * lax.rsqrt(ms + EPS) * gain_ref[...]).astype(o_ref.dtype)


def _sb_attention(sb, gain, tq):
    b, s, _ = sb.shape
    npair = SB_WIDTH // LANES
    gain2 = jnp.tile(gain.astype(F32), LANES // SB_HEAD_DIM).reshape(1, LANES)
    return pl.pallas_call(
        functools.partial(_sb_attn_kernel, tq=tq),
        grid=(b, npair, s // tq),
        in_specs=[
            pl.BlockSpec((1, tq, LANES), lambda bi, p, qi: (bi, qi, p)),
            pl.BlockSpec((1, s, LANES), lambda bi, p, qi: (bi, 0, npair + p)),
            pl.BlockSpec((1, s, LANES), lambda bi, p, qi: (bi, 0, 2 * npair + p)),
            pl.BlockSpec((1, LANES), lambda bi, p, qi: (0, 0)),
        ],
        out_specs=pl.BlockSpec((1, tq, LANES), lambda bi, p, qi: (bi, qi, p)),
        out_shape=jax.ShapeDtypeStruct((b, s, SB_WIDTH), BF16),
        scratch_shapes=[pltpu.VMEM((tq, LANES), F32)] * 3,
        compiler_params=_params("parallel", "parallel", "arbitrary"),
        name="sb_attention",
    )(sb, sb, sb, gain2)


_DONE = object()


def _block_diag(b, same_head):
    tiled = jnp.concatenate([b] * DN_HEADS, axis=0)
    return jnp.where(same_head, tiled, jnp.zeros_like(tiled))


def _hp_dot1(a, b, same_head):
    return _dot(a.astype(BF16), _block_diag(b.astype(BF16), same_head))


def _hp_dot3(a, b, same_head):
    ah, al = _split3(a)
    bh, bl = _split3(b)
    wh = _block_diag(bh, same_head)
    return _dot(ah, wh) + (_dot(ah, _block_diag(bl, same_head)) + _dot(al, wh))


def _hp_unit_lower_inverse(m, eye, same_sub, same_head):
    d = jnp.where(same_sub, m, 0.0)
    low = m - d
    p = eye - d
    dk = d
    for _ in range(DN_SUB.bit_length() - 2):
        dk = _hp_dot1(dk, dk, same_head)
        yield
        p = p + _hp_dot1(p, dk, same_head)
    yield
    n = _hp_dot1(p, low, same_head)
    yield
    t = eye - n
    nk = n
    for _ in range((DN_CHUNK // DN_SUB).bit_length() - 2):
        nk = _hp_dot1(nk, nk, same_head)
        yield
        t = t + _hp_dot1(t, nk, same_head)
    yield
    x0 = _hp_dot1(t, p, same_head)
    yield
    resid = eye - x0 - _hp_dot3(m, x0, same_head)
    yield
    return x0 + _hp_dot1(x0, resid, same_head)


def _gdn_kernel(qkv_ref, z_ref, ab_ref, cw_ref, alog_ref, dtb_ref, gain_ref, o_ref,
                xpad_ref, y_ref, u_ref, w_ref, qd_ref, kd_ref, attn_ref, cd_ref, state_ref, *, nc):
    c = DN_CHUNK
    hc = DN_HEADS * c
    tb = nc * c
    pad = SUBLANES

    @pl.when(pl.program_id(1) == 0)
    def _():
        state_ref[...] = jnp.zeros_like(state_ref)
        xpad_ref[0:pad, :] = jnp.zeros((pad, xpad_ref.shape[1]), F32)

    xpad_ref[pad:pad + tb, :] = qkv_ref[0]
    conv = xpad_ref[pl.ds(pad - DN_CONV + 1, tb), :] * cw_ref[0:1, :]
    for kk in range(1, DN_CONV):
        conv = conv + xpad_ref[pl.ds(pad - DN_CONV + 1 + kk, tb), :] * cw_ref[kk:kk + 1, :]
    xpad_ref[0:pad, :] = xpad_ref[tb:tb + pad, :]
    y_ref[...] = conv * _sigmoid(conv)

    prow = lax.broadcasted_iota(jnp.int32, (c, hc), 0)
    pcol = lax.broadcasted_iota(jnp.int32, (c, hc), 1) % c
    causal = prow >= pcol
    strict = prow > pcol
    eye = jnp.where(prow == pcol, 1.0, 0.0).astype(F32)
    same_sub = (prow // DN_SUB) == (pcol // DN_SUB)
    brow = lax.broadcasted_iota(jnp.int32, (hc, hc), 0) // c
    bcol = lax.broadcasted_iota(jnp.int32, (hc, hc), 1) // c
    same_head = brow == bcol
    krow = lax.broadcasted_iota(jnp.int32, (hc, DN_WIDTH), 0) // c
    kcol = lax.broadcasted_iota(jnp.int32, (hc, DN_WIDTH), 1) // DN_HEAD_DIM
    same_head_k = krow == kcol
    r16 = lax.broadcasted_iota(jnp.int32, (c, c), 0)
    c16 = lax.broadcasted_iota(jnp.int32, (c, c), 1)
    tril = jnp.where(r16 >= c16, 1.0, 0.0).astype(BF16)
    ones_cc = jnp.ones((c, c), BF16)
    sl = lax.broadcasted_iota(jnp.int32, (LANES, DN_WIDTH), 0)
    sh = lax.broadcasted_iota(jnp.int32, (LANES, DN_WIDTH), 1) // DN_HEAD_DIM
    sel_g_wide = jnp.where(sl == sh, 1.0, 0.0).astype(BF16)
    sel_b_wide = jnp.where(sl == sh + DN_HEADS, 1.0, 0.0).astype(BF16)
    pl_ = lax.broadcasted_iota(jnp.int32, (LANES, hc), 0)
    ph = lax.broadcasted_iota(jnp.int32, (LANES, hc), 1) // c
    sel_g_packed = jnp.where(pl_ == ph, 1.0, 0.0).astype(BF16)
    alog = alog_ref[...]
    dtb = dtb_ref[...]
    gain = gain_ref[...]

    def per_head(x, fn):
        return jnp.concatenate(
            [fn(x[:, h * DN_HEAD_DIM:(h + 1) * DN_HEAD_DIM]) for h in range(DN_HEADS)], axis=1)

    def l2_scale(x):
        inv = lax.rsqrt(jnp.sum(x * x, axis=-1, keepdims=True) + EPS)
        return jnp.broadcast_to(inv, x.shape)

    def prepare(ci):
        r0 = pl.multiple_of(ci * c, c)
        s0 = pl.multiple_of(ci * hc, hc)
        ab = ab_ref[0, pl.ds(r0, c), :]
        g_all = -jnp.exp(alog) * _softplus(ab + dtb)
        beta_all = _sigmoid(ab)
        gc_all = _dot_exact_lhs(tril, g_all)
        cd_ref[pl.ds(r0, c), :] = jnp.exp(gc_all)
        gcol_w = _dot_exact_rhs(gc_all, sel_g_wide)
        beta_w = _dot_exact_rhs(beta_all, sel_b_wide)
        gcol = _dot_exact_rhs(gc_all, sel_g_packed)
        grow = _dot_exact_lhs(ones_cc, gcol * eye)
        decay = jnp.where(causal, jnp.exp(jnp.where(causal, gcol - grow, 0.0)), 0.0)

        q = y_ref[pl.ds(r0, c), 0:DN_WIDTH]
        k = y_ref[pl.ds(r0, c), DN_WIDTH:2 * DN_WIDTH]
        v = y_ref[pl.ds(r0, c), 2 * DN_WIDTH:3 * DN_WIDTH]
        q = q * per_head(q, l2_scale) * (DN_HEAD_DIM ** -0.5)
        k = k * per_head(k, l2_scale)
        kb = k * beta_w
        k_bd = _block_diag_k(k.astype(BF16), same_head_k)
        kk_t = _dot_nt(kb.astype(BF16), k_bd)
        qk_t = _dot_nt(q.astype(BF16), k_bd)
        yield
        t_inv = yield from _hp_unit_lower_inverse(
            jnp.where(strict, kk_t * decay, 0.0), eye, same_sub, same_head)
        yield

        egc_w = jnp.exp(gcol_w)
        vb = v * beta_w
        kbe = kb * egc_w
        rhs = jnp.concatenate(
            [jnp.concatenate([vb[:, h * DN_HEAD_DIM:(h + 1) * DN_HEAD_DIM],
                              kbe[:, h * DN_HEAD_DIM:(h + 1) * DN_HEAD_DIM]], axis=1)
             for h in range(DN_HEADS)], axis=0)
        th, tl = _split3(_block_diag(t_inv, same_head))
        rh, rl = _split3(rhs)
        uw = _dot(th, rh) + (_dot(th, rl) + _dot(tl, rh))
        u_ref[pl.ds(s0, hc), :] = uw[:, :DN_HEAD_DIM]
        w_ref[pl.ds(s0, hc), :] = uw[:, DN_HEAD_DIM:].astype(BF16)
        attn_ref[pl.ds(s0, hc), :] = _block_diag((qk_t * decay).astype(BF16), same_head)
        qd_ref[pl.ds(r0, c), :] = (q * egc_w).astype(BF16)
        kd_ref[pl.ds(r0, c), :] = (k * jnp.exp(gcol_w[c - 1:c, :] - gcol_w)).astype(BF16)

    def prepare_group(gi, carry):
        running = [prepare(gi * DN_PREP_GROUP + j) for j in range(DN_PREP_GROUP)]
        while running:
            running = [g for g in running if next(g, _DONE) is not _DONE]
        return carry

    lax.fori_loop(0, nc // DN_PREP_GROUP, prepare_group, 0)

    def scan(ci, carry):
        r0 = pl.multiple_of(ci * c, c)
        s0 = pl.multiple_of(ci * hc, hc)
        cd_all = cd_ref[pl.ds(r0 + c - 1, 1), :]
        states = [state_ref[h] for h in range(DN_HEADS)]
        sbf = [s.astype(BF16) for s in states]
        v_new = [u_ref[pl.ds(s0 + h * c, c), :] - _dot(w_ref[pl.ds(s0 + h * c, c), :], sbf[h])
                 for h in range(DN_HEADS)]
        vnb = [x.astype(BF16) for x in v_new]
        o_intra = _dot(attn_ref[pl.ds(s0, hc), :], jnp.concatenate(vnb, axis=0))
        for h in range(DN_HEADS):
            hs = slice(h * DN_HEAD_DIM, (h + 1) * DN_HEAD_DIM)
            o = _dot(qd_ref[pl.ds(r0, c), hs], sbf[h]) + o_intra[h * c:(h + 1) * c, :]
            state_ref[h] = states[h] * cd_all[:, h:h + 1] + _dot_tn(kd_ref[pl.ds(r0, c), hs], vnb[h])
            zz = z_ref[0, pl.ds(r0, c), hs]
            on = o * lax.rsqrt(jnp.mean(o * o, axis=-1, keepdims=True) + EPS) * gain
            o_ref[0, pl.ds(r0, c), hs] = (on * (zz * _sigmoid(zz))).astype(o_ref.dtype)
        return carry

    lax.fori_loop(0, nc, scan, 0)


def _block_diag_k(k_bf16, same_head_k):
    tiled = jnp.concatenate([k_bf16] * DN_HEADS, axis=0)
    return jnp.where(same_head_k, tiled, jnp.zeros_like(tiled))


def _gdn(rest, conv_w, a_log, dt_bias, gain, tb):
    b, s, _ = rest.shape
    nc = tb // DN_CHUNK
    wq = 3 * DN_WIDTH
    hrows = DN_HEADS * tb

    def lanes(vec, offset):
        return jnp.zeros((1, LANES), F32).at[0, offset:offset + vec.shape[0]].set(vec.astype(F32))

    return pl.pallas_call(
        functools.partial(_gdn_kernel, nc=nc),
        grid=(b, s // tb),
        in_specs=[
            pl.BlockSpec((1, tb, wq), lambda bi, t: (bi, t, 0)),
            pl.BlockSpec((1, tb, DN_WIDTH), lambda bi, t: (bi, t, wq // DN_WIDTH)),
            pl.BlockSpec((1, tb, LANES), lambda bi, t: (bi, t, _AB_COL // LANES)),
            pl.BlockSpec((DN_CONV, wq), lambda bi, t: (0, 0)),
            pl.BlockSpec((1, LANES), lambda bi, t: (0, 0)),
            pl.BlockSpec((1, LANES), lambda bi, t: (0, 0)),
            pl.BlockSpec((1, DN_HEAD_DIM), lambda bi, t: (0, 0)),
        ],
        out_specs=pl.BlockSpec((1, tb, DN_WIDTH), lambda bi, t: (bi, t, 0)),
        out_shape=jax.ShapeDtypeStruct((b, s, DN_WIDTH), BF16),
        scratch_shapes=[
            pltpu.VMEM((tb + 2 * SUBLANES, wq), F32),
            pltpu.VMEM((tb, wq), F32),
            pltpu.VMEM((hrows, DN_HEAD_DIM), F32),
            pltpu.VMEM((hrows, DN_HEAD_DIM), BF16),
            pltpu.VMEM((tb, DN_WIDTH), BF16),
            pltpu.VMEM((tb, DN_WIDTH), BF16),
            pltpu.VMEM((hrows, DN_HEADS * DN_CHUNK), BF16),
            pltpu.VMEM((tb, LANES), F32),
            pltpu.VMEM((DN_HEADS, DN_HEAD_DIM, DN_HEAD_DIM), F32),
        ],
        compiler_params=_params("parallel", "arbitrary"),
        name="gated_deltanet",
    )(rest, rest, rest, conv_w.astype(F32), lanes(a_log, 0), lanes(dt_bias, 0),
      gain.astype(F32).reshape(1, DN_HEAD_DIM))


def _hgrn2_kernel(q_ref, f_ref, i_ref, g_ref, lb_ref, gain_ref, o_ref, acc_ref, state_ref, *, nsub):
    c = HG_CHUNK
    w = HG_WIDTH

    @pl.when(pl.program_id(1) == 0)
    def _():
        state_ref[...] = jnp.zeros_like(state_ref)

    row = lax.broadcasted_iota(jnp.int32, (w, w), 0)
    col = lax.broadcasted_iota(jnp.int32, (w, w), 1)
    same_head = (row // HG_DIM) == (col // HG_DIM)
    head_ones = jnp.where(same_head, 1.0, 0.0).astype(BF16)
    r16 = lax.broadcasted_iota(jnp.int32, (c, c), 0)
    c16 = lax.broadcasted_iota(jnp.int32, (c, c), 1)
    tril = jnp.where(r16 >= c16, 1.0, 0.0).astype(BF16)
    trow = lax.broadcasted_iota(jnp.int32, (c, w), 0)
    lb = lb_ref[...]
    log_lb = jnp.log(lb)
    log_1mlb = jnp.log(1.0 - lb)

    def sub(si, carry):
        r0 = pl.multiple_of(si * c, c)
        hq = q_ref[0, pl.ds(r0, c), :]
        hf = f_ref[0, pl.ds(r0, c), :]
        v = i_ref[0, pl.ds(r0, c), :]
        q = hq * _sigmoid(hq)
        log_sig = jnp.minimum(hf, 0.0) - jnp.log(1.0 + jnp.exp(-jnp.abs(hf)))
        bterm = log_1mlb + log_sig
        mx = jnp.maximum(log_lb, bterm)
        lf = mx + jnp.log(jnp.exp(log_lb - mx) + jnp.exp(bterm - mx))
        k = (1.0 - lb) * _sigmoid(-hf)
        bcum = _dot_exact_lhs(tril, lf)
        last = bcum[c - 1:c, :]
        st = state_ref[...]
        o = _dot_nt((q * jnp.exp(bcum)).astype(BF16), st.astype(BF16))
        xs = []
        for s_i in range(c):
            keep = trow >= s_i
            dec = jnp.where(keep, jnp.exp(jnp.where(keep, bcum - bcum[s_i:s_i + 1, :], 0.0)), 0.0)
            xs.append((dec * q * k[s_i:s_i + 1, :]).astype(BF16))
        scores = _dot(jnp.concatenate(xs, axis=0), head_ones)
        for s_i in range(c):
            o = o + scores[s_i * c:(s_i + 1) * c, :] * v[s_i:s_i + 1, :]
        acc_ref[pl.ds(r0, c), :] = o
        kd = k * jnp.exp(last - bcum)
        upd = _dot_tn(v.astype(BF16), kd.astype(BF16))
        state_ref[...] = jnp.where(same_head, st * jnp.exp(last) + upd, 0.0)
        return carry

    lax.fori_loop(0, nsub, sub, 0)

    o = acc_ref[...]
    ms = _dot_exact_rhs(o * o, head_ones) * (1.0 / HG_DIM)
    gate = g_ref[0]
    o_ref[0] = (o * lax.rsqrt(ms + EPS) * gain_ref[...] * _sigmoid(gate)).astype(o_ref.dtype)


def _hgrn2(rest, lb, gain, tb):
    b, s, _ = rest.shape
    cb = _HG_COL // HG_WIDTH
    blk = lambda j: pl.BlockSpec((1, tb, HG_WIDTH), lambda bi, t: (bi, t, cb + j))
    vec = pl.BlockSpec((1, HG_WIDTH), lambda bi, t: (0, 0))
    return pl.pallas_call(
        functools.partial(_hgrn2_kernel, nsub=tb // HG_CHUNK),
        grid=(b, s // tb),
        in_specs=[blk(0), blk(1), blk(2), blk(3), vec, vec],
        out_specs=pl.BlockSpec((1, tb, HG_WIDTH), lambda bi, t: (bi, t, 0)),
        out_shape=jax.ShapeDtypeStruct((b, s, HG_WIDTH), BF16),
        scratch_shapes=[pltpu.VMEM((tb, HG_WIDTH), F32), pltpu.VMEM((HG_WIDTH, HG_WIDTH), F32)],
        compiler_params=_params("parallel", "arbitrary"),
        name="hgrn2",
    )(rest, rest, rest, rest, lb.astype(F32).reshape(1, HG_WIDTH),
      jnp.tile(gain.astype(F32), HG_HEADS).reshape(1, HG_WIDTH))


def _rms(x, gain):
    return x * lax.rsqrt(jnp.mean(x * x, axis=-1, keepdims=True) + EPS) * gain


def _out_ffn_kernel(x_ref, sb_ref, dn_ref, hg_ref, wsb_ref, wdn_ref, whg_ref, n2_ref,
                    wg_ref, wu_ref, wd_ref, fn_ref, o_ref, *, final):
    x1 = x_ref[...] + _dot(sb_ref[...], wsb_ref[...]) + _dot(dn_ref[...], wdn_ref[...]) \
        + _dot(hg_ref[...], whg_ref[...])
    h2 = _rms(x1, n2_ref[...]).astype(BF16)
    g = _dot(h2, wg_ref[...])
    u = _dot(h2, wu_ref[...])
    ff = (g * _sigmoid(g) * u).astype(BF16)
    out = x1 + _dot(ff, wd_ref[...])
    if final:
        out = _rms(out, fn_ref[...])
    o_ref[...] = out


def _out_ffn(x, sb_o, dn_o, hg_o, w_out, norm2, w_gate, w_up, w_down, final_norm, final, tm):
    m, d = x.shape
    rows = lambda width: pl.BlockSpec((tm, width), lambda i: (i, 0))
    whole = lambda a: pl.BlockSpec(a.shape, lambda i: (0,) * a.ndim, pipeline_mode=pl.Buffered(1))
    wsb = w_out[:SB_WIDTH]
    wdn = w_out[SB_WIDTH:SB_WIDTH + DN_WIDTH]
    whg = w_out[SB_WIDTH + DN_WIDTH:]
    n2 = norm2.astype(F32).reshape(1, d)
    fn = final_norm.astype(F32).reshape(1, d)
    args = (x, sb_o, dn_o, hg_o, wsb, wdn, whg, n2, w_gate, w_up, w_down, fn)
    in_specs = [rows(d), rows(SB_WIDTH), rows(DN_WIDTH), rows(HG_WIDTH)] + [whole(a) for a in args[4:]]
    return pl.pallas_call(
        functools.partial(_out_ffn_kernel, final=final),
        grid=(m // tm,),
        in_specs=in_specs,
        out_specs=rows(d),
        out_shape=jax.ShapeDtypeStruct((m, d), F32),
        compiler_params=_params("parallel"),
        name="out_ffn",
    )(*args)


_HG_COL = 3 * DN_WIDTH + DN_WIDTH
_AB_COL = _HG_COL + 4 * HG_WIDTH
_REST_WIDTH = _AB_COL + LANES


def _split_w_in(w_in):
    sb = w_in[:, :3 * SB_WIDTH]
    o = 3 * SB_WIDTH
    dn_qkvz = w_in[:, o:o + 4 * DN_WIDTH]
    o += 4 * DN_WIDTH
    dn_ab = w_in[:, o:o + 2 * DN_HEADS]
    o += 2 * DN_HEADS
    hg = w_in[:, o:]
    pad = jnp.zeros((w_in.shape[0], LANES - 2 * DN_HEADS), w_in.dtype)
    rest = jnp.concatenate([dn_qkvz, hg, dn_ab, pad], axis=1)
    return sb.astype(BF16), rest.astype(BF16)


def kernel(x, norm1, w_in, sb_gain, dn_conv, dn_a_log, dn_dt_bias, dn_gain, hg_lb_logits, hg_gain,
           w_out, norm2, w_gate, w_up, w_down, final_norm):
    b, s, d = x.shape
    depth = w_in.shape[0]
    lbs = jnp.cumsum(jax.nn.softmax(hg_lb_logits.astype(F32), axis=0), axis=0)
    lbs = lbs - lbs[0:1]
    xf = x.reshape(b * s, d)
    for l in range(depth):
        w_sb, w_rest = _split_w_in(w_in[l])
        sb = _norm_matmul(xf, norm1[l].astype(F32), w_sb, BF16, 512, 3 * SB_WIDTH)
        rest = _norm_matmul(xf, norm1[l].astype(F32), w_rest, F32, 512, 640)
        sb_o = _sb_attention(sb.reshape(b, s, -1), sb_gain[l], 512)
        rest3 = rest.reshape(b, s, -1)
        dn_o = _gdn(rest3, dn_conv[l], dn_a_log[l], dn_dt_bias[l], dn_gain[l], 512)
        hg_o = _hgrn2(rest3, lbs[l], hg_gain[l], 256)
        xf = _out_ffn(xf, sb_o.reshape(b * s, -1), dn_o.reshape(b * s, -1), hg_o.reshape(b * s, -1),
                      w_out[l].astype(BF16), norm2[l], w_gate[l].astype(BF16), w_up[l].astype(BF16),
                      w_down[l].astype(BF16), final_norm, l == depth - 1, 256)
    return xf.reshape(b, s, d)
```

```python
import functools

import jax
import jax.numpy as jnp
from jax import lax
from jax.experimental import pallas as pl
from jax.experimental.pallas import tpu as pltpu

EPS = 1e-6
F32 = jnp.float32
BF16 = jnp.bfloat16
LOG2E = 1.4426950408889634

D_MODEL = 1024
SB_HEAD_DIM = 64
SB_WIDTH = 256
SB_UNROLL = 4
DN_HEAD_DIM = 128
DN_HEADS = 4
DN_WIDTH = 512
DN_CONV = 4
DN_CHUNK = 64
DN_SUB = 16
DN_PREP_GROUP = 8
HG_HEADS = 4
HG_DIM = 64
HG_WIDTH = 256
HG_CHUNK = 16
HG_GROUP = 4
LANES = 128
SUBLANES = 8
VMEM_LIMIT_BYTES = 56 * 1024 * 1024


def _params(*sem):
    return pltpu.CompilerParams(dimension_semantics=sem, vmem_limit_bytes=VMEM_LIMIT_BYTES)


def _split3(a):
    hi = a.astype(BF16)
    lo = (a - hi.astype(F32)).astype(BF16)
    return hi, lo


def _dot(a, b):
    return jnp.dot(a, b, preferred_element_type=F32)


def _dot_nt(a, b):
    return lax.dot_general(a, b, (((1,), (1,)), ((), ())), preferred_element_type=F32)


def _dot_tn(a, b):
    return lax.dot_general(a, b, (((0,), (0,)), ((), ())), preferred_element_type=F32)


def _dot_exact_rhs(a, b_bf16):
    ah, al = _split3(a)
    return _dot(ah, b_bf16) + _dot(al, b_bf16)


def _dot_exact_lhs(a_bf16, b):
    bh, bl = _split3(b)
    return _dot(a_bf16, bh) + _dot(a_bf16, bl)


def _softplus(x):
    neg_abs = lax.bitcast_convert_type(
        lax.bitcast_convert_type(x, jnp.uint32) | jnp.uint32(0x80000000), F32)
    return jnp.maximum(x, 0.0) + jnp.log(1.0 + jnp.exp2(neg_abs * LOG2E))


def _softplus2(x2):
    neg_abs = lax.bitcast_convert_type(
        lax.bitcast_convert_type(x2, jnp.uint32) | jnp.uint32(0x80000000), F32)
    return jnp.maximum(x2, 0.0) + jnp.log(1.0 + jnp.exp2(neg_abs)) * LOG2E


def _sigmoid(x):
    return 1.0 / (1.0 + jnp.exp(-x))


def _in_proj_kernel(x_ref, g_ref, wsb_ref, wrest_ref, sbscale_ref, sb_ref, rest_ref):
    x = x_ref[...]
    y = (x * lax.rsqrt(jnp.mean(x * x, axis=-1, keepdims=True) + EPS) * g_ref[...]).astype(BF16)
    sb_ref[...] = (_dot(y, wsb_ref[...]) * sbscale_ref[...]).astype(sb_ref.dtype)
    rest_ref[...] = _dot(y, wrest_ref[...])


def _in_proj(x, gain, w_sb, w_rest, tm):
    m, d = x.shape
    nsb, nrest = w_sb.shape[1], w_rest.shape[1]
    sbscale = jnp.concatenate([jnp.full((1, SB_WIDTH), SB_HEAD_DIM ** -0.5 * LOG2E, F32),
                               jnp.ones((1, nsb - SB_WIDTH), F32)], axis=1)
    whole = lambda a: pl.BlockSpec(a.shape, lambda i: (0,) * a.ndim, pipeline_mode=pl.Buffered(1))
    gain = gain.astype(F32).reshape(1, d)
    return pl.pallas_call(
        _in_proj_kernel,
        grid=(m // tm,),
        in_specs=[pl.BlockSpec((tm, d), lambda i: (i, 0)), whole(gain), whole(w_sb), whole(w_rest),
                  whole(sbscale)],
        out_specs=[pl.BlockSpec((tm, nsb), lambda i: (i, 0)), pl.BlockSpec((tm, nrest), lambda i: (i, 0))],
        out_shape=[jax.ShapeDtypeStruct((m, nsb), BF16), jax.ShapeDtypeStruct((m, nrest), F32)],
        compiler_params=_params("parallel"),
        name="in_proj",
    )(x, gain, w_sb, w_rest, sbscale)


def _sb_attn_kernel(q_ref, k_ref, v_ref, gain_ref, o_ref, acc_ref, off0_ref, off1_ref,
                    kx_ref, vx_ref, *, tq):
    tk = LANES
    ndiag = tq // tk
    qi = pl.program_id(2)
    q = q_ref[0]

    @pl.when(qi == 0)
    def _():
        head0 = lax.broadcasted_iota(jnp.int32, (tk, LANES), 1) < SB_HEAD_DIM

        def expand(j, carry):
            ks = pl.multiple_of(j * tk, tk)
            for src, dst in ((k_ref, kx_ref), (v_ref, vx_ref)):
                x = src[0, pl.ds(ks, tk), :]
                zero = jnp.zeros_like(x)
                dst[j] = jnp.concatenate([jnp.where(head0, x, zero), jnp.where(head0, zero, x)], axis=0)
            return carry

        lax.fori_loop(0, kx_ref.shape[0], expand, 0)

    row = lax.broadcasted_iota(jnp.int32, (2 * tk, 2 * tk), 0)
    col = lax.broadcasted_iota(jnp.int32, (2 * tk, 2 * tk), 1)
    rev = jnp.where((row >= col) & ((row >= tk) == (col >= tk)), 1.0, 0.0).astype(BF16)
    qrow = lax.broadcasted_iota(jnp.int32, (tq, 2 * tk), 0)
    kcol = lax.broadcasted_iota(jnp.int32, (tq, 2 * tk), 1)
    kcol = jnp.where(kcol >= tk, kcol - tk, kcol)

    acc_ref[...] = jnp.zeros_like(acc_ref)
    off0_ref[...] = jnp.zeros_like(off0_ref)
    off1_ref[...] = jnp.zeros_like(off1_ref)

    def scores(kj, past):
        z = _dot_nt(q, kx_ref[kj])
        sp = _softplus2(z)
        if past is not None:
            sp = jnp.where(past, sp, 0.0)
        tot0 = jnp.broadcast_to(jnp.sum(sp[:, :tk], axis=1, keepdims=True), (tq, LANES))
        tot1 = jnp.broadcast_to(jnp.sum(sp[:, tk:], axis=1, keepdims=True), (tq, LANES))
        return z, sp.astype(BF16), tot0, tot1

    def accumulate(kj, past, staged, off):
        z, spb, tot0, tot1 = staged
        cum = _dot(spb, rev)
        w = jnp.exp2(z - cum - jnp.concatenate(off, axis=1))
        if past is not None:
            w = jnp.where(past, w, 0.0)
        acc_ref[...] += _dot(w.astype(BF16), vx_ref[kj])
        return off[0] + tot0, off[1] + tot1

    def walk(tiles):
        off = (off0_ref[...], off1_ref[...])
        staged = scores(*tiles[0])
        for (kj, past), nxt in zip(tiles, tiles[1:] + [None]):
            ahead = scores(*nxt) if nxt is not None else None
            off = accumulate(kj, past, staged, off)
            staged = ahead
        off0_ref[...], off1_ref[...] = off

    walk([(qi * ndiag + d, kcol + d * tk < qrow) for d in range(ndiag - 1, -1, -1)])

    def body(it, carry):
        kj = qi * ndiag - 1 - SB_UNROLL * it
        walk([(kj - u, None) for u in range(SB_UNROLL)])
        return carry

    lax.fori_loop(0, qi * (ndiag // SB_UNROLL), body, 0)

    head0q = lax.broadcasted_iota(jnp.int32, (tq, LANES), 1) < SB_HEAD_DIM
    o = acc_ref[...]
    oo = o * o
    ms0 = jnp.sum(jnp.where(head0q, oo, 0.0), axis=1, keepdims=True)
    ms1 = jnp.sum(jnp.where(head0q, 0.0, oo), axis=1, keepdims=True)
    ms = jnp.where(head0q, ms0, ms1) * (1.0 / SB_HEAD_DIM)
    o_ref[0] = (o * lax.rsqrt(ms + EPS) * gain_ref[...]).astype(o_ref.dtype)


def _sb_attention(sb, gain, tq):
    b, s, _ = sb.shape
    npair = SB_WIDTH // LANES
    gain2 = jnp.tile(gain.astype(F32), LANES // SB_HEAD_DIM).reshape(1, LANES)
    return pl.pallas_call(
        functools.partial(_sb_attn_kernel, tq=tq),
        grid=(b, npair, s // tq),
        in_specs=[
            pl.BlockSpec((1, tq, LANES), lambda bi, p, qi: (bi, qi, p)),
            pl.BlockSpec((1, s, LANES), lambda bi, p, qi: (bi, 0, npair + p)),
            pl.BlockSpec((1, s, LANES), lambda bi, p, qi: (bi, 0, 2 * npair + p)),
            pl.BlockSpec((1, LANES), lambda bi, p, qi: (0, 0)),
        ],
        out_specs=pl.BlockSpec((1, tq, LANES), lambda bi, p, qi: (bi, qi, p)),
        out_shape=jax.ShapeDtypeStruct((b, s, SB_WIDTH), BF16),
        scratch_shapes=[pltpu.VMEM((tq, LANES), F32)] * 3
        + [pltpu.VMEM((s // LANES, 2 * LANES, LANES), BF16)] * 2,
        compiler_params=_params("parallel", "parallel", "arbitrary"),
        name="sb_attention",
    )(sb, sb, sb, gain2)


_DONE = object()


def _block_diag(b, same_head):
    tiled = jnp.concatenate([b] * DN_HEADS, axis=0)
    return jnp.where(same_head, tiled, jnp.zeros_like(tiled))


def _hp_dot1(a, b, same_head):
    return _dot(a.astype(BF16), _block_diag(b.astype(BF16), same_head))


def _hp_dot3(a, b, same_head):
    ah, al = _split3(a)
    bh, bl = _split3(b)
    wh = _block_diag(bh, same_head)
    return _dot(ah, wh) + (_dot(ah, _block_diag(bl, same_head)) + _dot(al, wh))


def _hp_unit_lower_inverse(m, eye, same_sub, same_head):
    d = jnp.where(same_sub, m, 0.0)
    low = m - d
    p = eye - d
    dk = d
    for _ in range(DN_SUB.bit_length() - 2):
        dk = _hp_dot1(dk, dk, same_head)
        yield
        p = p + _hp_dot1(p, dk, same_head)
    yield
    n = _hp_dot1(p, low, same_head)
    yield
    t = eye - n
    nk = n
    for _ in range((DN_CHUNK // DN_SUB).bit_length() - 2):
        nk = _hp_dot1(nk, nk, same_head)
        yield
        t = t + _hp_dot1(t, nk, same_head)
    yield
    x0 = _hp_dot1(t, p, same_head)
    yield
    resid = eye - x0 - _hp_dot3(m, x0, same_head)
    yield
    return x0 + _hp_dot1(x0, resid, same_head)


def _gdn_kernel(qkv_ref, z_ref, ab_ref, cw_ref, alog_ref, dtb_ref, gain_ref, o_ref,
                xpad_ref, y_ref, u_ref, w_ref, qd_ref, kd_ref, attn_ref, cd_ref, state_ref, *, nc):
    c = DN_CHUNK
    hc = DN_HEADS * c
    tb = nc * c
    pad = SUBLANES

    @pl.when(pl.program_id(1) == 0)
    def _():
        state_ref[...] = jnp.zeros_like(state_ref)
        xpad_ref[0:pad, :] = jnp.zeros((pad, xpad_ref.shape[1]), F32)

    xpad_ref[pad:pad + tb, :] = qkv_ref[0]
    conv = xpad_ref[pl.ds(pad - DN_CONV + 1, tb), :] * cw_ref[0:1, :]
    for kk in range(1, DN_CONV):
        conv = conv + xpad_ref[pl.ds(pad - DN_CONV + 1 + kk, tb), :] * cw_ref[kk:kk + 1, :]
    xpad_ref[0:pad, :] = xpad_ref[tb:tb + pad, :]
    y_ref[...] = conv * _sigmoid(conv)

    prow = lax.broadcasted_iota(jnp.int32, (c, hc), 0)
    pcol = lax.broadcasted_iota(jnp.int32, (c, hc), 1) % c
    causal = prow >= pcol
    strict = prow > pcol
    eye = jnp.where(prow == pcol, 1.0, 0.0).astype(F32)
    same_sub = (prow // DN_SUB) == (pcol // DN_SUB)
    brow = lax.broadcasted_iota(jnp.int32, (hc, hc), 0) // c
    bcol = lax.broadcasted_iota(jnp.int32, (hc, hc), 1) // c
    same_head = brow == bcol
    krow = lax.broadcasted_iota(jnp.int32, (hc, DN_WIDTH), 0) // c
    kcol = lax.broadcasted_iota(jnp.int32, (hc, DN_WIDTH), 1) // DN_HEAD_DIM
    same_head_k = krow == kcol
    r16 = lax.broadcasted_iota(jnp.int32, (c, c), 0)
    c16 = lax.broadcasted_iota(jnp.int32, (c, c), 1)
    tril = jnp.where(r16 >= c16, 1.0, 0.0).astype(BF16)
    ones_cc = jnp.ones((c, c), BF16)
    sl = lax.broadcasted_iota(jnp.int32, (LANES, DN_WIDTH), 0)
    sh = lax.broadcasted_iota(jnp.int32, (LANES, DN_WIDTH), 1) // DN_HEAD_DIM
    sel_g_wide = jnp.where(sl == sh, 1.0, 0.0).astype(BF16)
    sel_b_wide = jnp.where(sl == sh + DN_HEADS, 1.0, 0.0).astype(BF16)
    pl_ = lax.broadcasted_iota(jnp.int32, (LANES, hc), 0)
    ph = lax.broadcasted_iota(jnp.int32, (LANES, hc), 1) // c
    sel_g_packed = jnp.where(pl_ == ph, 1.0, 0.0).astype(BF16)
    alog = alog_ref[...]
    dtb = dtb_ref[...]
    gain = gain_ref[...]

    def per_head(x, fn):
        return jnp.concatenate(
            [fn(x[:, h * DN_HEAD_DIM:(h + 1) * DN_HEAD_DIM]) for h in range(DN_HEADS)], axis=1)

    def l2_scale(x):
        inv = lax.rsqrt(jnp.sum(x * x, axis=-1, keepdims=True) + EPS)
        return jnp.broadcast_to(inv, x.shape)

    def prepare(ci):
        r0 = pl.multiple_of(ci * c, c)
        s0 = pl.multiple_of(ci * hc, hc)
        ab = ab_ref[0, pl.ds(r0, c), :]
        g_all = -jnp.exp(alog) * _softplus(ab + dtb)
        beta_all = _sigmoid(ab)
        gc_all = _dot_exact_lhs(tril, g_all)
        cd_ref[pl.ds(r0, c), :] = jnp.exp(gc_all)
        gcol_w = _dot_exact_rhs(gc_all, sel_g_wide)
        beta_w = _dot_exact_rhs(beta_all, sel_b_wide)
        gcol = _dot_exact_rhs(gc_all, sel_g_packed)
        grow = _dot_exact_lhs(ones_cc, gcol * eye)
        decay = jnp.where(causal, jnp.exp(jnp.where(causal, gcol - grow, 0.0)), 0.0)

        q = y_ref[pl.ds(r0, c), 0:DN_WIDTH]
        k = y_ref[pl.ds(r0, c), DN_WIDTH:2 * DN_WIDTH]
        v = y_ref[pl.ds(r0, c), 2 * DN_WIDTH:3 * DN_WIDTH]
        q = q * per_head(q, l2_scale) * (DN_HEAD_DIM ** -0.5)
        k = k * per_head(k, l2_scale)
        kb = k * beta_w
        k_bd = _block_diag_k(k.astype(BF16), same_head_k)
        kk_t = _dot_nt(kb.astype(BF16), k_bd)
        qk_t = _dot_nt(q.astype(BF16), k_bd)
        yield
        t_inv = yield from _hp_unit_lower_inverse(
            jnp.where(strict, kk_t * decay, 0.0), eye, same_sub, same_head)
        yield

        egc_w = jnp.exp(gcol_w)
        vb = v * beta_w
        kbe = kb * egc_w
        rhs = jnp.concatenate(
            [jnp.concatenate([vb[:, h * DN_HEAD_DIM:(h + 1) * DN_HEAD_DIM],
                              kbe[:, h * DN_HEAD_DIM:(h + 1) * DN_HEAD_DIM]], axis=1)
             for h in range(DN_HEADS)], axis=0)
        th, tl = _split3(_block_diag(t_inv, same_head))
        rh, rl = _split3(rhs)
        uw = _dot(th, rh) + (_dot(th, rl) + _dot(tl, rh))
        u_ref[pl.ds(s0, hc), :] = uw[:, :DN_HEAD_DIM]
        w_ref[pl.ds(s0, hc), :] = uw[:, DN_HEAD_DIM:].astype(BF16)
        attn_ref[pl.ds(s0, hc), :] = _block_diag((qk_t * decay).astype(BF16), same_head)
        qd_ref[pl.ds(r0, c), :] = (q * egc_w).astype(BF16)
        kd_ref[pl.ds(r0, c), :] = (k * jnp.exp(gcol_w[c - 1:c, :] - gcol_w)).astype(BF16)

    def prepare_group(gi, carry):
        running = [prepare(gi * DN_PREP_GROUP + j) for j in range(DN_PREP_GROUP)]
        while running:
            running = [g for g in running if next(g, _DONE) is not _DONE]
        return carry

    lax.fori_loop(0, nc // DN_PREP_GROUP, prepare_group, 0)

    def scan(ci, carry):
        r0 = pl.multiple_of(ci * c, c)
        s0 = pl.multiple_of(ci * hc, hc)
        cd_all = cd_ref[pl.ds(r0 + c - 1, 1), :]
        states = [state_ref[h] for h in range(DN_HEADS)]
        sbf = [s.astype(BF16) for s in states]
        v_new = [u_ref[pl.ds(s0 + h * c, c), :] - _dot(w_ref[pl.ds(s0 + h * c, c), :], sbf[h])
                 for h in range(DN_HEADS)]
        vnb = [x.astype(BF16) for x in v_new]
        o_intra = _dot(attn_ref[pl.ds(s0, hc), :], jnp.concatenate(vnb, axis=0))
        for h in range(DN_HEADS):
            hs = slice(h * DN_HEAD_DIM, (h + 1) * DN_HEAD_DIM)
            o = _dot(qd_ref[pl.ds(r0, c), hs], sbf[h]) + o_intra[h * c:(h + 1) * c, :]
            state_ref[h] = states[h] * cd_all[:, h:h + 1] + _dot_tn(kd_ref[pl.ds(r0, c), hs], vnb[h])
            zz = z_ref[0, pl.ds(r0, c), hs]
            on = o * lax.rsqrt(jnp.mean(o * o, axis=-1, keepdims=True) + EPS) * gain
            o_ref[0, pl.ds(r0, c), hs] = (on * (zz * _sigmoid(zz))).astype(o_ref.dtype)
        return carry

    lax.fori_loop(0, nc, scan, 0)


def _block_diag_k(k_bf16, same_head_k):
    tiled = jnp.concatenate([k_bf16] * DN_HEADS, axis=0)
    return jnp.where(same_head_k, tiled, jnp.zeros_like(tiled))


def _gdn(rest, conv_w, a_log, dt_bias, gain, tb):
    b, s, _ = rest.shape
    nc = tb // DN_CHUNK
    wq = 3 * DN_WIDTH
    hrows = DN_HEADS * tb

    def lanes(vec, offset):
        return jnp.zeros((1, LANES), F32).at[0, offset:offset + vec.shape[0]].set(vec.astype(F32))

    return pl.pallas_call(
        functools.partial(_gdn_kernel, nc=nc),
        grid=(b, s // tb),
        in_specs=[
            pl.BlockSpec((1, tb, wq), lambda bi, t: (bi, t, 0)),
            pl.BlockSpec((1, tb, DN_WIDTH), lambda bi, t: (bi, t, wq // DN_WIDTH)),
            pl.BlockSpec((1, tb, LANES), lambda bi, t: (bi, t, _AB_COL // LANES)),
            pl.BlockSpec((DN_CONV, wq), lambda bi, t: (0, 0)),
            pl.BlockSpec((1, LANES), lambda bi, t: (0, 0)),
            pl.BlockSpec((1, LANES), lambda bi, t: (0, 0)),
            pl.BlockSpec((1, DN_HEAD_DIM), lambda bi, t: (0, 0)),
        ],
        out_specs=pl.BlockSpec((1, tb, DN_WIDTH), lambda bi, t: (bi, t, 0)),
        out_shape=jax.ShapeDtypeStruct((b, s, DN_WIDTH), BF16),
        scratch_shapes=[
            pltpu.VMEM((tb + 2 * SUBLANES, wq), F32),
            pltpu.VMEM((tb, wq), F32),
            pltpu.VMEM((hrows, DN_HEAD_DIM), F32),
            pltpu.VMEM((hrows, DN_HEAD_DIM), BF16),
            pltpu.VMEM((tb, DN_WIDTH), BF16),
            pltpu.VMEM((tb, DN_WIDTH), BF16),
            pltpu.VMEM((hrows, DN_HEADS * DN_CHUNK), BF16),
            pltpu.VMEM((tb, LANES), F32),
            pltpu.VMEM((DN_HEADS, DN_HEAD_DIM, DN_HEAD_DIM), F32),
        ],
        compiler_params=_params("parallel", "arbitrary"),
        name="gated_deltanet",
    )(rest, rest, rest, conv_w.astype(F32), lanes(a_log, 0), lanes(dt_bias, 0),
      gain.astype(F32).reshape(1, DN_HEAD_DIM))


def _hgrn2_kernel(q_ref, f_ref, i_ref, g_ref, lb_ref, gain_ref, o_ref, acc_ref, state_ref, *, nsub):
    c = HG_CHUNK
    w = HG_WIDTH

    @pl.when(pl.program_id(1) == 0)
    def _():
        state_ref[...] = jnp.zeros_like(state_ref)

    row = lax.broadcasted_iota(jnp.int32, (w, w), 0)
    col = lax.broadcasted_iota(jnp.int32, (w, w), 1)
    same_head = (row // HG_DIM) == (col // HG_DIM)
    head_ones = jnp.where(same_head, 1.0, 0.0).astype(BF16)
    r16 = lax.broadcasted_iota(jnp.int32, (c, c), 0)
    c16 = lax.broadcasted_iota(jnp.int32, (c, c), 1)
    tril = jnp.where(r16 >= c16, 1.0, 0.0).astype(BF16)
    trow = lax.broadcasted_iota(jnp.int32, (c, w), 0)
    lb = lb_ref[...]
    log_lb = jnp.log(lb)
    log_1mlb = jnp.log(1.0 - lb)

    def sub(si, run):
        r0 = pl.multiple_of(si * c, c)
        hq = q_ref[0, pl.ds(r0, c), :]
        hf = f_ref[0, pl.ds(r0, c), :]
        v = i_ref[0, pl.ds(r0, c), :]
        q = hq * _sigmoid(hq)
        log_sig = jnp.minimum(hf, 0.0) - jnp.log(1.0 + jnp.exp(-jnp.abs(hf)))
        bterm = log_1mlb + log_sig
        mx = jnp.maximum(log_lb, bterm)
        lf = mx + jnp.log(jnp.exp(log_lb - mx) + jnp.exp(bterm - mx))
        k = (1.0 - lb) * _sigmoid(-hf)
        bcum = _dot_exact_lhs(tril, lf)
        last = bcum[c - 1:c, :]
        q_dec = (q * jnp.exp(bcum)).astype(BF16)
        xs = []
        for s_i in range(c):
            keep = trow >= s_i
            dec = jnp.where(keep, jnp.exp(jnp.where(keep, bcum - bcum[s_i:s_i + 1, :], 0.0)), 0.0)
            xs.append((dec * q * k[s_i:s_i + 1, :]).astype(BF16))
        scores = _dot(jnp.concatenate(xs, axis=0), head_ones)
        intra = scores[0:c, :] * v[0:1, :]
        for s_i in range(1, c):
            intra = intra + scores[s_i * c:(s_i + 1) * c, :] * v[s_i:s_i + 1, :]
        kd = k * jnp.exp(last - bcum)
        upd = _dot_tn(v.astype(BF16), kd.astype(BF16))
        e_last = jnp.exp(last)
        yield
        st = run["state"]
        acc_ref[pl.ds(r0, c), :] = _dot_nt(q_dec, st.astype(BF16)) + intra
        run["state"] = jnp.where(same_head, st * e_last + upd, 0.0)

    def sub_group(gi, carry):
        run = {"state": state_ref[...]}
        running = [sub(gi * HG_GROUP + j, run) for j in range(HG_GROUP)]
        while running:
            running = [g for g in running if next(g, _DONE) is not _DONE]
        state_ref[...] = run["state"]
        return carry

    lax.fori_loop(0, nsub // HG_GROUP, sub_group, 0)

    o = acc_ref[...]
    ms = _dot_exact_rhs(o * o, head_ones) * (1.0 / HG_DIM)
    gate = g_ref[0]
    o_ref[0] = (o * lax.rsqrt(ms + EPS) * gain_ref[...] * _sigmoid(gate)).astype(o_ref.dtype)


def _hgrn2(rest, lb, gain, tb):
    b, s, _ = rest.shape
    cb = _HG_COL // HG_WIDTH
    blk = lambda j: pl.BlockSpec((1, tb, HG_WIDTH), lambda bi, t: (bi, t, cb + j))
    vec = pl.BlockSpec((1, HG_WIDTH), lambda bi, t: (0, 0))
    return pl.pallas_call(
        functools.partial(_hgrn2_kernel, nsub=tb // HG_CHUNK),
        grid=(b, s // tb),
        in_specs=[blk(0), blk(1), blk(2), blk(3), vec, vec],
        out_specs=pl.BlockSpec((1, tb, HG_WIDTH), lambda bi, t: (bi, t, 0)),
        out_shape=jax.ShapeDtypeStruct((b, s, HG_WIDTH), BF16),
        scratch_shapes=[pltpu.VMEM((tb, HG_WIDTH), F32), pltpu.VMEM((HG_WIDTH, HG_WIDTH), F32)],
        compiler_params=_params("parallel", "arbitrary"),
        name="hgrn2",
    )(rest, rest, rest, rest, lb.astype(F32).reshape(1, HG_WIDTH),
      jnp.tile(gain.astype(F32), HG_HEADS).reshape(1, HG_WIDTH))


def _rms(x, gain):
    return x * lax.rsqrt(jnp.mean(x * x, axis=-1, keepdims=True) + EPS) * gain


def _out_ffn_kernel(x_ref, sb_ref, dn_ref, hg_ref, wsb_ref, wdn_ref, whg_ref, n2_ref,
                    wg_ref, wu_ref, wd_ref, fn_ref, o_ref, *, final):
    x1 = x_ref[...] + _dot(sb_ref[...], wsb_ref[...]) + _dot(dn_ref[...], wdn_ref[...]) \
        + _dot(hg_ref[...], whg_ref[...])
    h2 = _rms(x1, n2_ref[...]).astype(BF16)
    g = _dot(h2, wg_ref[...])
    u = _dot(h2, wu_ref[...])
    ff = (g * _sigmoid(g) * u).astype(BF16)
    out = x1 + _dot(ff, wd_ref[...])
    if final:
        out = _rms(out, fn_ref[...])
    o_ref[...] = out


def _out_ffn(x, sb_o, dn_o, hg_o, w_out, norm2, w_gate, w_up, w_down, final_norm, final, tm):
    m, d = x.shape
    rows = lambda width: pl.BlockSpec((tm, width), lambda i: (i, 0))
    whole = lambda a: pl.BlockSpec(a.shape, lambda i: (0,) * a.ndim, pipeline_mode=pl.Buffered(1))
    wsb = w_out[:SB_WIDTH]
    wdn = w_out[SB_WIDTH:SB_WIDTH + DN_WIDTH]
    whg = w_out[SB_WIDTH + DN_WIDTH:]
    n2 = norm2.astype(F32).reshape(1, d)
    fn = final_norm.astype(F32).reshape(1, d)
    args = (x, sb_o, dn_o, hg_o, wsb, wdn, whg, n2, w_gate, w_up, w_down, fn)
    in_specs = [rows(d), rows(SB_WIDTH), rows(DN_WIDTH), rows(HG_WIDTH)] + [whole(a) for a in args[4:]]
    return pl.pallas_call(
        functools.partial(_out_ffn_kernel, final=final),
        grid=(m // tm,),
        in_specs=in_specs,
        out_specs=rows(d),
        out_shape=jax.ShapeDtypeStruct((m, d), F32),
        compiler_params=_params("parallel"),
        name="out_ffn",
    )(*args)


_HG_COL = 3 * DN_WIDTH + DN_WIDTH
_AB_COL = _HG_COL + 4 * HG_WIDTH
_REST_WIDTH = _AB_COL + LANES


def _split_w_in(w_in):
    sb = w_in[:, :3 * SB_WIDTH]
    o = 3 * SB_WIDTH
    dn_qkvz = w_in[:, o:o + 4 * DN_WIDTH]
    o += 4 * DN_WIDTH
    dn_ab = w_in[:, o:o + 2 * DN_HEADS]
    o += 2 * DN_HEADS
    hg = w_in[:, o:]
    pad = jnp.zeros((w_in.shape[0], LANES - 2 * DN_HEADS), w_in.dtype)
    rest = jnp.concatenate([dn_qkvz, hg, dn_ab, pad], axis=1)
    return sb.astype(BF16), rest.astype(BF16)


def kernel(x, norm1, w_in, sb_gain, dn_conv, dn_a_log, dn_dt_bias, dn_gain, hg_lb_logits, hg_gain,
           w_out, norm2, w_gate, w_up, w_down, final_norm):
    b, s, d = x.shape
    depth = w_in.shape[0]
    lbs = jnp.cumsum(jax.nn.softmax(hg_lb_logits.astype(F32), axis=0), axis=0)
    lbs = lbs - lbs[0:1]
    xf = x.reshape(b * s, d)
    for l in range(depth):
        w_sb, w_rest = _split_w_in(w_in[l])
        sb, rest = _in_proj(xf, norm1[l], w_sb, w_rest, 512)
        sb_o = _sb_attention(sb.reshape(b, s, -1), sb_gain[l], 1024)
        rest3 = rest.reshape(b, s, -1)
        dn_o = _gdn(rest3, dn_conv[l], dn_a_log[l], dn_dt_bias[l], dn_gain[l], 512)
        hg_o = _hgrn2(rest3, lbs[l], hg_gain[l], 256)
        xf = _out_ffn(xf, sb_o.reshape(b * s, -1), dn_o.reshape(b * s, -1), hg_o.reshape(b * s, -1),
                      w_out[l].astype(BF16), norm2[l], w_gate[l].astype(BF16), w_up[l].astype(BF16),
                      w_down[l].astype(BF16), final_norm, l == depth - 1, 256)
    return xf.reshape(b, s, d)
```

```python
import functools

import jax
import jax.numpy as jnp
from jax import lax
from jax.experimental import pallas as pl
from jax.experimental.pallas import tpu as pltpu

EPS = 1e-6
F32 = jnp.float32
BF16 = jnp.bfloat16
LOG2E = 1.4426950408889634

SB_HEAD_DIM = 64
SB_WIDTH = 256
SB_UNROLL = 8
DN_HEAD_DIM = 128
DN_HEADS = 4
DN_WIDTH = 512
DN_CONV = 4
DN_CHUNK = 64
DN_SUB = 16
DN_PREP_GROUP = 8
HG_HEADS = 4
HG_DIM = 64
HG_WIDTH = 256
HG_CHUNK = 16
HG_GROUP = 8
LANES = 128
SUBLANES = 8
VMEM_LIMIT_BYTES = 56 * 1024 * 1024

IN_PROJ_ROWS = 512
SB_QUERY_ROWS = 1024
DN_BLOCK_ROWS = 512
HG_BLOCK_ROWS = 256
FFN_ROWS = 512

_DONE = object()


def _params(*sem):
    return pltpu.CompilerParams(dimension_semantics=sem, vmem_limit_bytes=VMEM_LIMIT_BYTES)


def _split3(a):
    hi = a.astype(BF16)
    lo = (a - hi.astype(F32)).astype(BF16)
    return hi, lo


def _dot(a, b):
    return jnp.dot(a, b, preferred_element_type=F32)


def _dot_nt(a, b):
    return lax.dot_general(a, b, (((1,), (1,)), ((), ())), preferred_element_type=F32)


def _dot_tn(a, b):
    return lax.dot_general(a, b, (((0,), (0,)), ((), ())), preferred_element_type=F32)


def _dot_exact_rhs(a, b_bf16):
    ah, al = _split3(a)
    return _dot(ah, b_bf16) + _dot(al, b_bf16)


def _dot_exact_lhs(a_bf16, b):
    bh, bl = _split3(b)
    return _dot(a_bf16, bh) + _dot(a_bf16, bl)


def _softplus(x):
    neg_abs = lax.bitcast_convert_type(
        lax.bitcast_convert_type(x, jnp.uint32) | jnp.uint32(0x80000000), F32)
    return jnp.maximum(x, 0.0) + jnp.log(1.0 + jnp.exp2(neg_abs * LOG2E))


def _softplus2(x2):
    neg_abs = lax.bitcast_convert_type(
        lax.bitcast_convert_type(x2, jnp.uint32) | jnp.uint32(0x80000000), F32)
    return jnp.maximum(x2, 0.0) + jnp.log(1.0 + jnp.exp2(neg_abs)) * LOG2E


def _sigmoid(x):
    return 1.0 / (1.0 + jnp.exp(-x))


def _in_proj_kernel(x_ref, g_ref, wsb_ref, wrest_ref, sbscale_ref, sb_ref, rest_ref):
    x = x_ref[...]
    y = (x * lax.rsqrt(jnp.mean(x * x, axis=-1, keepdims=True) + EPS) * g_ref[...]).astype(BF16)
    sb_ref[...] = (_dot(y, wsb_ref[...]) * sbscale_ref[...]).astype(sb_ref.dtype)
    rest_ref[...] = _dot(y, wrest_ref[...])


def _in_proj(x, gain, w_sb, w_rest, tm):
    m, d = x.shape
    nsb, nrest = w_sb.shape[1], w_rest.shape[1]
    sbscale = jnp.concatenate([jnp.full((1, SB_WIDTH), SB_HEAD_DIM ** -0.5 * LOG2E, F32),
                               jnp.ones((1, nsb - SB_WIDTH), F32)], axis=1)
    whole = lambda a: pl.BlockSpec(a.shape, lambda i: (0,) * a.ndim, pipeline_mode=pl.Buffered(1))
    gain = gain.astype(F32).reshape(1, d)
    return pl.pallas_call(
        _in_proj_kernel,
        grid=(m // tm,),
        in_specs=[pl.BlockSpec((tm, d), lambda i: (i, 0)), whole(gain), whole(w_sb), whole(w_rest),
                  whole(sbscale)],
        out_specs=[pl.BlockSpec((tm, nsb), lambda i: (i, 0)), pl.BlockSpec((tm, nrest), lambda i: (i, 0))],
        out_shape=[jax.ShapeDtypeStruct((m, nsb), BF16), jax.ShapeDtypeStruct((m, nrest), F32)],
        compiler_params=_params("parallel"),
        name="in_proj",
    )(x, gain, w_sb, w_rest, sbscale)


def _sb_attn_kernel(q_ref, k_ref, v_ref, gain_ref, o_ref, acc_ref, off0_ref, off1_ref,
                    kx_ref, vx_ref, *, tq):
    tk = LANES
    ndiag = tq // tk
    qi = pl.program_id(2)
    q = q_ref[0]

    @pl.when(qi == 0)
    def _():
        head0 = lax.broadcasted_iota(jnp.int32, (tk, LANES), 1) < SB_HEAD_DIM

        def expand(j, carry):
            ks = pl.multiple_of(j * tk, tk)
            for src, dst in ((k_ref, kx_ref), (v_ref, vx_ref)):
                x = src[0, pl.ds(ks, tk), :]
                zero = jnp.zeros_like(x)
                dst[j] = jnp.concatenate([jnp.where(head0, x, zero), jnp.where(head0, zero, x)], axis=0)
            return carry

        lax.fori_loop(0, kx_ref.shape[0], expand, 0)

    row = lax.broadcasted_iota(jnp.int32, (2 * tk, 2 * tk), 0)
    col = lax.broadcasted_iota(jnp.int32, (2 * tk, 2 * tk), 1)
    rev = jnp.where((row >= col) & ((row >= tk) == (col >= tk)), 1.0, 0.0).astype(BF16)
    qrow = lax.broadcasted_iota(jnp.int32, (tq, 2 * tk), 0)
    kcol = lax.broadcasted_iota(jnp.int32, (tq, 2 * tk), 1)
    kcol = jnp.where(kcol >= tk, kcol - tk, kcol)

    acc_ref[...] = jnp.zeros_like(acc_ref)
    off0_ref[...] = jnp.zeros_like(off0_ref)
    off1_ref[...] = jnp.zeros_like(off1_ref)

    def scores(kj, past, r0):
        z = _dot_nt(q[r0:], kx_ref[kj])
        sp = _softplus2(z)
        if past is not None:
            sp = jnp.where(past[r0:], sp, 0.0)
        tot0 = jnp.broadcast_to(jnp.sum(sp[:, :tk], axis=1, keepdims=True), (tq - r0, LANES))
        tot1 = jnp.broadcast_to(jnp.sum(sp[:, tk:], axis=1, keepdims=True), (tq - r0, LANES))
        return z, sp.astype(BF16), tot0, tot1

    def accumulate(kj, past, r0, staged, off):
        z, spb, tot0, tot1 = staged
        cum = _dot(spb, rev)
        w = jnp.exp2(z - cum - jnp.concatenate([off[0][r0:], off[1][r0:]], axis=1))
        if past is not None:
            w = jnp.where(past[r0:], w, 0.0)
        acc_ref[r0:, :] += _dot(w.astype(BF16), vx_ref[kj])
        if r0 == 0:
            return off[0] + tot0, off[1] + tot1
        return (jnp.concatenate([off[0][:r0], off[0][r0:] + tot0], axis=0),
                jnp.concatenate([off[1][:r0], off[1][r0:] + tot1], axis=0))

    def walk(tiles):
        off = (off0_ref[...], off1_ref[...])
        staged = scores(*tiles[0])
        for tile, nxt in zip(tiles, tiles[1:] + [None]):
            ahead = scores(*nxt) if nxt is not None else None
            off = accumulate(*tile, staged, off)
            staged = ahead
        off0_ref[...], off1_ref[...] = off

    walk([(qi * ndiag + d, kcol + d * tk < qrow, d * tk) for d in range(ndiag - 1, -1, -1)])

    def body(it, carry):
        kj = qi * ndiag - 1 - SB_UNROLL * it
        walk([(kj - u, None, 0) for u in range(SB_UNROLL)])
        return carry

    lax.fori_loop(0, qi * (ndiag // SB_UNROLL), body, 0)

    head0q = lax.broadcasted_iota(jnp.int32, (tq, LANES), 1) < SB_HEAD_DIM
    o = acc_ref[...]
    oo = o * o
    ms0 = jnp.sum(jnp.where(head0q, oo, 0.0), axis=1, keepdims=True)
    ms1 = jnp.sum(jnp.where(head0q, 0.0, oo), axis=1, keepdims=True)
    ms = jnp.where(head0q, ms0, ms1) * (1.0 / SB_HEAD_DIM)
    o_ref[0] = (o * lax.rsqrt(ms + EPS) * gain_ref[...]).astype(o_ref.dtype)


def _sb_attention(sb, gain, tq):
    b, s, _ = sb.shape
    npair = SB_WIDTH // LANES
    gain2 = jnp.tile(gain.astype(F32), LANES // SB_HEAD_DIM).reshape(1, LANES)
    return pl.pallas_call(
        functools.partial(_sb_attn_kernel, tq=tq),
        grid=(b, npair, s // tq),
        in_specs=[
            pl.BlockSpec((1, tq, LANES), lambda bi, p, qi: (bi, qi, p)),
            pl.BlockSpec((1, s, LANES), lambda bi, p, qi: (bi, 0, npair + p)),
            pl.BlockSpec((1, s, LANES), lambda bi, p, qi: (bi, 0, 2 * npair + p)),
            pl.BlockSpec((1, LANES), lambda bi, p, qi: (0, 0)),
        ],
        out_specs=pl.BlockSpec((1, tq, LANES), lambda bi, p, qi: (bi, qi, p)),
        out_shape=jax.ShapeDtypeStruct((b, s, SB_WIDTH), BF16),
        scratch_shapes=[pltpu.VMEM((tq, LANES), F32)] * 3
        + [pltpu.VMEM((s // LANES, 2 * LANES, LANES), BF16)] * 2,
        compiler_params=_params("parallel", "parallel", "arbitrary"),
        name="sb_attention",
    )(sb, sb, sb, gain2)


def _block_diag(b, same_head):
    tiled = jnp.concatenate([b] * DN_HEADS, axis=0)
    return jnp.where(same_head, tiled, jnp.zeros_like(tiled))


def _hp_dot1(a, b, same_head):
    return _dot(a.astype(BF16), _block_diag(b.astype(BF16), same_head))


def _hp_dot3(a, b, same_head):
    ah, al = _split3(a)
    bh, bl = _split3(b)
    wh = _block_diag(bh, same_head)
    return _dot(ah, wh) + (_dot(ah, _block_diag(bl, same_head)) + _dot(al, wh))


def _hp_unit_lower_inverse(m, eye, same_sub, same_head):
    d = jnp.where(same_sub, m, 0.0)
    low = m - d
    p = eye - d
    dk = d
    for _ in range(DN_SUB.bit_length() - 2):
        dk = _hp_dot1(dk, dk, same_head)
        yield
        p = p + _hp_dot1(p, dk, same_head)
    yield
    n = _hp_dot1(p, low, same_head)
    yield
    t = eye - n
    nk = n
    for _ in range((DN_CHUNK // DN_SUB).bit_length() - 2):
        nk = _hp_dot1(nk, nk, same_head)
        yield
        t = t + _hp_dot1(t, nk, same_head)
    yield
    x0 = _hp_dot1(t, p, same_head)
    yield
    resid = eye - x0 - _hp_dot3(m, x0, same_head)
    yield
    return x0 + _hp_dot1(x0, resid, same_head)


def _gdn_kernel(qkv_ref, z_ref, ab_ref, cw_ref, alog_ref, dtb_ref, gain_ref, o_ref,
                xpad_ref, y_ref, u_ref, w_ref, qd_ref, kd_ref, attn_ref, cd_ref, state_ref, *, nc):
    nb = qkv_ref.shape[0]
    c = DN_CHUNK
    hc = DN_HEADS * c
    tb = nc * c
    pad = SUBLANES

    @pl.when(pl.program_id(0) == 0)
    def _():
        state_ref[...] = jnp.zeros_like(state_ref)
        for bi in range(nb):
            xpad_ref[bi, 0:pad, :] = jnp.zeros((pad, xpad_ref.shape[2]), F32)

    for bi in range(nb):
        xpad_ref[bi, pad:pad + tb, :] = qkv_ref[bi]
        conv = xpad_ref[bi, pl.ds(pad - DN_CONV + 1, tb), :] * cw_ref[0:1, :]
        for kk in range(1, DN_CONV):
            conv = conv + xpad_ref[bi, pl.ds(pad - DN_CONV + 1 + kk, tb), :] * cw_ref[kk:kk + 1, :]
        xpad_ref[bi, 0:pad, :] = xpad_ref[bi, tb:tb + pad, :]
        y_ref[bi * tb:(bi + 1) * tb, :] = conv * _sigmoid(conv)

    prow = lax.broadcasted_iota(jnp.int32, (c, hc), 0)
    pcol = lax.broadcasted_iota(jnp.int32, (c, hc), 1) % c
    causal = prow >= pcol
    strict = prow > pcol
    eye = jnp.where(prow == pcol, 1.0, 0.0).astype(F32)
    same_sub = (prow // DN_SUB) == (pcol // DN_SUB)
    brow = lax.broadcasted_iota(jnp.int32, (hc, hc), 0) // c
    bcol = lax.broadcasted_iota(jnp.int32, (hc, hc), 1) // c
    same_head = brow == bcol
    krow = lax.broadcasted_iota(jnp.int32, (hc, DN_WIDTH), 0) // c
    kcol = lax.broadcasted_iota(jnp.int32, (hc, DN_WIDTH), 1) // DN_HEAD_DIM
    same_head_k = krow == kcol
    r16 = lax.broadcasted_iota(jnp.int32, (c, c), 0)
    c16 = lax.broadcasted_iota(jnp.int32, (c, c), 1)
    tril = jnp.where(r16 >= c16, 1.0, 0.0).astype(BF16)
    ones_cc = jnp.ones((c, c), BF16)
    sl = lax.broadcasted_iota(jnp.int32, (LANES, DN_WIDTH), 0)
    sh = lax.broadcasted_iota(jnp.int32, (LANES, DN_WIDTH), 1) // DN_HEAD_DIM
    sel_g_wide = jnp.where(sl == sh, 1.0, 0.0).astype(BF16)
    sel_b_wide = jnp.where(sl == sh + DN_HEADS, 1.0, 0.0).astype(BF16)
    pl_ = lax.broadcasted_iota(jnp.int32, (LANES, hc), 0)
    ph = lax.broadcasted_iota(jnp.int32, (LANES, hc), 1) // c
    sel_g_packed = jnp.where(pl_ == ph, 1.0, 0.0).astype(BF16)
    alog = alog_ref[...]
    dtb = dtb_ref[...]
    gain = gain_ref[...]

    def per_head(x, fn):
        return jnp.concatenate(
            [fn(x[:, h * DN_HEAD_DIM:(h + 1) * DN_HEAD_DIM]) for h in range(DN_HEADS)], axis=1)

    def l2_scale(x):
        inv = lax.rsqrt(jnp.sum(x * x, axis=-1, keepdims=True) + EPS)
        return jnp.broadcast_to(inv, x.shape)

    def prepare(ci, bi):
        ab = ab_ref[bi, pl.ds(pl.multiple_of(ci * c, c), c), :]
        r0 = pl.multiple_of(bi * tb + ci * c, c)
        s0 = pl.multiple_of((bi * nc + ci) * hc, hc)
        g_all = -jnp.exp(alog) * _softplus(ab + dtb)
        beta_all = _sigmoid(ab)
        gc_all = _dot_exact_lhs(tril, g_all)
        cd_ref[pl.ds(r0, c), :] = jnp.exp(gc_all)
        gcol_w = _dot_exact_rhs(gc_all, sel_g_wide)
        beta_w = _dot_exact_rhs(beta_all, sel_b_wide)
        gcol = _dot_exact_rhs(gc_all, sel_g_packed)
        grow = _dot_exact_lhs(ones_cc, gcol * eye)
        decay = jnp.where(causal, jnp.exp(jnp.where(causal, gcol - grow, 0.0)), 0.0)

        q = y_ref[pl.ds(r0, c), 0:DN_WIDTH]
        k = y_ref[pl.ds(r0, c), DN_WIDTH:2 * DN_WIDTH]
        v = y_ref[pl.ds(r0, c), 2 * DN_WIDTH:3 * DN_WIDTH]
        q = q * per_head(q, l2_scale) * (DN_HEAD_DIM ** -0.5)
        k = k * per_head(k, l2_scale)
        kb = k * beta_w
        k_bd = _block_diag_k(k.astype(BF16), same_head_k)
        kk_t = _dot_nt(kb.astype(BF16), k_bd)
        qk_t = _dot_nt(q.astype(BF16), k_bd)
        yield
        t_inv = yield from _hp_unit_lower_inverse(
            jnp.where(strict, kk_t * decay, 0.0), eye, same_sub, same_head)
        yield

        egc_w = jnp.exp(gcol_w)
        vb = v * beta_w
        kbe = kb * egc_w
        rhs = jnp.concatenate(
            [jnp.concatenate([vb[:, h * DN_HEAD_DIM:(h + 1) * DN_HEAD_DIM],
                              kbe[:, h * DN_HEAD_DIM:(h + 1) * DN_HEAD_DIM]], axis=1)
             for h in range(DN_HEADS)], axis=0)
        th, tl = _split3(_block_diag(t_inv, same_head))
        rh, rl = _split3(rhs)
        uw = _dot(th, rh) + (_dot(th, rl) + _dot(tl, rh))
        u_ref[pl.ds(s0, hc), :] = uw[:, :DN_HEAD_DIM]
        w_ref[pl.ds(s0, hc), :] = uw[:, DN_HEAD_DIM:].astype(BF16)
        attn_ref[pl.ds(s0, hc), :] = _block_diag((qk_t * decay).astype(BF16), same_head)
        qd_ref[pl.ds(r0, c), :] = (q * egc_w).astype(BF16)
        kd_ref[pl.ds(r0, c), :] = (k * jnp.exp(gcol_w[c - 1:c, :] - gcol_w)).astype(BF16)

    def run_interleaved(gens):
        while gens:
            gens = [g for g in gens if next(g, _DONE) is not _DONE]

    for bi in range(nb):
        def prepare_group(gi, carry, bi=bi):
            run_interleaved([prepare(gi * DN_PREP_GROUP + j, bi) for j in range(DN_PREP_GROUP)])
            return carry

        lax.fori_loop(0, nc // DN_PREP_GROUP, prepare_group, 0)

    def scan(ci, bi):
        t0 = pl.multiple_of(ci * c, c)
        r0 = pl.multiple_of(bi * tb + ci * c, c)
        s0 = pl.multiple_of((bi * nc + ci) * hc, hc)
        cd_all = cd_ref[pl.ds(r0 + c - 1, 1), :]
        states = [state_ref[bi * DN_HEADS + h] for h in range(DN_HEADS)]
        sbf = [s.astype(BF16) for s in states]
        v_new = [u_ref[pl.ds(s0 + h * c, c), :] - _dot(w_ref[pl.ds(s0 + h * c, c), :], sbf[h])
                 for h in range(DN_HEADS)]
        yield
        vnb = [x.astype(BF16) for x in v_new]
        o_intra = _dot(attn_ref[pl.ds(s0, hc), :], jnp.concatenate(vnb, axis=0))
        for h in range(DN_HEADS):
            hs = slice(h * DN_HEAD_DIM, (h + 1) * DN_HEAD_DIM)
            o = _dot(qd_ref[pl.ds(r0, c), hs], sbf[h]) + o_intra[h * c:(h + 1) * c, :]
            state_ref[bi * DN_HEADS + h] = states[h] * cd_all[:, h:h + 1] + _dot_tn(
                kd_ref[pl.ds(r0, c), hs], vnb[h])
            zz = z_ref[bi, pl.ds(t0, c), hs]
            on = o * lax.rsqrt(jnp.mean(o * o, axis=-1, keepdims=True) + EPS) * gain
            o_ref[bi, pl.ds(t0, c), hs] = (on * (zz * _sigmoid(zz))).astype(o_ref.dtype)

    def scan_step(ci, carry):
        run_interleaved([scan(ci, bi) for bi in range(nb)])
        return carry

    lax.fori_loop(0, nc, scan_step, 0)


def _block_diag_k(k_bf16, same_head_k):
    tiled = jnp.concatenate([k_bf16] * DN_HEADS, axis=0)
    return jnp.where(same_head_k, tiled, jnp.zeros_like(tiled))


def _gdn(rest, conv_w, a_log, dt_bias, gain, tb):
    b, s, _ = rest.shape
    nc = tb // DN_CHUNK
    wq = 3 * DN_WIDTH
    hrows = DN_HEADS * tb

    def lanes(vec, offset):
        return jnp.zeros((1, LANES), F32).at[0, offset:offset + vec.shape[0]].set(vec.astype(F32))

    return pl.pallas_call(
        functools.partial(_gdn_kernel, nc=nc),
        grid=(s // tb,),
        in_specs=[
            pl.BlockSpec((b, tb, wq), lambda t: (0, t, 0)),
            pl.BlockSpec((b, tb, DN_WIDTH), lambda t: (0, t, wq // DN_WIDTH)),
            pl.BlockSpec((b, tb, LANES), lambda t: (0, t, _AB_COL // LANES)),
            pl.BlockSpec((DN_CONV, wq), lambda t: (0, 0)),
            pl.BlockSpec((1, LANES), lambda t: (0, 0)),
            pl.BlockSpec((1, LANES), lambda t: (0, 0)),
            pl.BlockSpec((1, DN_HEAD_DIM), lambda t: (0, 0)),
        ],
        out_specs=pl.BlockSpec((b, tb, DN_WIDTH), lambda t: (0, t, 0)),
        out_shape=jax.ShapeDtypeStruct((b, s, DN_WIDTH), BF16),
        scratch_shapes=[
            pltpu.VMEM((b, tb + 2 * SUBLANES, wq), F32),
            pltpu.VMEM((b * tb, wq), F32),
            pltpu.VMEM((b * hrows, DN_HEAD_DIM), F32),
            pltpu.VMEM((b * hrows, DN_HEAD_DIM), BF16),
            pltpu.VMEM((b * tb, DN_WIDTH), BF16),
            pltpu.VMEM((b * tb, DN_WIDTH), BF16),
            pltpu.VMEM((b * hrows, DN_HEADS * DN_CHUNK), BF16),
            pltpu.VMEM((b * tb, LANES), F32),
            pltpu.VMEM((b * DN_HEADS, DN_HEAD_DIM, DN_HEAD_DIM), F32),
        ],
        compiler_params=_params("arbitrary"),
        name="gated_deltanet",
    )(rest, rest, rest, conv_w.astype(F32), lanes(a_log, 0), lanes(dt_bias, 0),
      gain.astype(F32).reshape(1, DN_HEAD_DIM))


def _hgrn2_kernel(q_ref, f_ref, i_ref, g_ref, lb_ref, gain_ref, o_ref, acc_ref, state_ref, *, nsub):
    c = HG_CHUNK
    w = HG_WIDTH

    @pl.when(pl.program_id(1) == 0)
    def _():
        state_ref[...] = jnp.zeros_like(state_ref)

    row = lax.broadcasted_iota(jnp.int32, (w, w), 0)
    col = lax.broadcasted_iota(jnp.int32, (w, w), 1)
    same_head = (row // HG_DIM) == (col // HG_DIM)
    head_ones = jnp.where(same_head, 1.0, 0.0).astype(BF16)
    r16 = lax.broadcasted_iota(jnp.int32, (c, c), 0)
    c16 = lax.broadcasted_iota(jnp.int32, (c, c), 1)
    tril = jnp.where(r16 >= c16, 1.0, 0.0).astype(BF16)
    trow = lax.broadcasted_iota(jnp.int32, (c, w), 0)
    lb = lb_ref[...]
    log_lb = jnp.log(lb)
    log_1mlb = jnp.log(1.0 - lb)

    def sub(si, run):
        r0 = pl.multiple_of(si * c, c)
        hq = q_ref[0, pl.ds(r0, c), :]
        hf = f_ref[0, pl.ds(r0, c), :]
        v = i_ref[0, pl.ds(r0, c), :]
        q = hq * _sigmoid(hq)
        log_sig = jnp.minimum(hf, 0.0) - jnp.log(1.0 + jnp.exp(-jnp.abs(hf)))
        bterm = log_1mlb + log_sig
        mx = jnp.maximum(log_lb, bterm)
        lf = mx + jnp.log(jnp.exp(log_lb - mx) + jnp.exp(bterm - mx))
        k = (1.0 - lb) * _sigmoid(-hf)
        bcum = _dot_exact_lhs(tril, lf)
        last = bcum[c - 1:c, :]
        q_dec = (q * jnp.exp(bcum)).astype(BF16)
        xs = []
        for s_i in range(c):
            keep = trow >= s_i
            dec = jnp.where(keep, jnp.exp(jnp.where(keep, bcum - bcum[s_i:s_i + 1, :], 0.0)), 0.0)
            xs.append((dec * q * k[s_i:s_i + 1, :]).astype(BF16))
        scores = _dot(jnp.concatenate(xs, axis=0), head_ones)
        intra = scores[0:c, :] * v[0:1, :]
        for s_i in range(1, c):
            intra = intra + scores[s_i * c:(s_i + 1) * c, :] * v[s_i:s_i + 1, :]
        kd = k * jnp.exp(last - bcum)
        upd = _dot_tn(v.astype(BF16), kd.astype(BF16))
        e_last = jnp.exp(last)
        yield
        st = run["state"]
        acc_ref[pl.ds(r0, c), :] = _dot_nt(q_dec, st.astype(BF16)) + intra
        run["state"] = jnp.where(same_head, st * e_last + upd, 0.0)

    def sub_group(gi, carry):
        run = {"state": state_ref[...]}
        running = [sub(gi * HG_GROUP + j, run) for j in range(HG_GROUP)]
        while running:
            running = [g for g in running if next(g, _DONE) is not _DONE]
        state_ref[...] = run["state"]
        return carry

    lax.fori_loop(0, nsub // HG_GROUP, sub_group, 0)

    o = acc_ref[...]
    ms = _dot_exact_rhs(o * o, head_ones) * (1.0 / HG_DIM)
    gate = g_ref[0]
    o_ref[0] = (o * lax.rsqrt(ms + EPS) * gain_ref[...] * _sigmoid(gate)).astype(o_ref.dtype)


def _hgrn2(rest, lb, gain, tb):
    b, s, _ = rest.shape
    cb = _HG_COL // HG_WIDTH
    blk = lambda j: pl.BlockSpec((1, tb, HG_WIDTH), lambda bi, t: (bi, t, cb + j))
    vec = pl.BlockSpec((1, HG_WIDTH), lambda bi, t: (0, 0))
    return pl.pallas_call(
        functools.partial(_hgrn2_kernel, nsub=tb // HG_CHUNK),
        grid=(b, s // tb),
        in_specs=[blk(0), blk(1), blk(2), blk(3), vec, vec],
        out_specs=pl.BlockSpec((1, tb, HG_WIDTH), lambda bi, t: (bi, t, 0)),
        out_shape=jax.ShapeDtypeStruct((b, s, HG_WIDTH), BF16),
        scratch_shapes=[
            pltpu.VMEM((tb, HG_WIDTH), F32),
            pltpu.VMEM((HG_WIDTH, HG_WIDTH), F32),
        ],
        compiler_params=_params("parallel", "arbitrary"),
        name="hgrn2",
    )(rest, rest, rest, rest, lb.astype(F32).reshape(1, HG_WIDTH),
      jnp.tile(gain.astype(F32), HG_HEADS).reshape(1, HG_WIDTH))


def _rms(x, gain):
    return x * lax.rsqrt(jnp.mean(x * x, axis=-1, keepdims=True) + EPS) * gain


def _out_ffn_kernel(x_ref, sb_ref, dn_ref, hg_ref, wsb_ref, wdn_ref, whg_ref, n2_ref,
                    wg_ref, wu_ref, wd_ref, fn_ref, o_ref, *, final):
    x1 = x_ref[...] + _dot(sb_ref[...], wsb_ref[...]) + _dot(dn_ref[...], wdn_ref[...]) \
        + _dot(hg_ref[...], whg_ref[...])
    h2 = _rms(x1, n2_ref[...]).astype(BF16)
    g = _dot(h2, wg_ref[...])
    u = _dot(h2, wu_ref[...])
    ff = (g * _sigmoid(g) * u).astype(BF16)
    out = x1 + _dot(ff, wd_ref[...])
    if final:
        out = _rms(out, fn_ref[...])
    o_ref[...] = out


def _out_ffn(x, sb_o, dn_o, hg_o, w_out, norm2, w_gate, w_up, w_down, final_norm, final, tm):
    m, d = x.shape
    rows = lambda width: pl.BlockSpec((tm, width), lambda i: (i, 0))
    whole = lambda a: pl.BlockSpec(a.shape, lambda i: (0,) * a.ndim, pipeline_mode=pl.Buffered(1))
    wsb = w_out[:SB_WIDTH]
    wdn = w_out[SB_WIDTH:SB_WIDTH + DN_WIDTH]
    whg = w_out[SB_WIDTH + DN_WIDTH:]
    n2 = norm2.astype(F32).reshape(1, d)
    fn = final_norm.astype(F32).reshape(1, d)
    args = (x, sb_o, dn_o, hg_o, wsb, wdn, whg, n2, w_gate, w_up, w_down, fn)
    in_specs = [rows(d), rows(SB_WIDTH), rows(DN_WIDTH), rows(HG_WIDTH)] + [whole(a) for a in args[4:]]
    return pl.pallas_call(
        functools.partial(_out_ffn_kernel, final=final),
        grid=(m // tm,),
        in_specs=in_specs,
        out_specs=rows(d),
        out_shape=jax.ShapeDtypeStruct((m, d), F32),
        compiler_params=_params("parallel"),
        name="out_ffn",
    )(*args)


_HG_COL = 3 * DN_WIDTH + DN_WIDTH
_AB_COL = _HG_COL + 4 * HG_WIDTH


def _split_w_in(w_in):
    sb = w_in[:, :3 * SB_WIDTH]
    o = 3 * SB_WIDTH
    dn_qkvz = w_in[:, o:o + 4 * DN_WIDTH]
    o += 4 * DN_WIDTH
    dn_ab = w_in[:, o:o + 2 * DN_HEADS]
    o += 2 * DN_HEADS
    hg = w_in[:, o:]
    pad = jnp.zeros((w_in.shape[0], LANES - 2 * DN_HEADS), w_in.dtype)
    rest = jnp.concatenate([dn_qkvz, hg, dn_ab, pad], axis=1)
    return sb.astype(BF16), rest.astype(BF16)


def kernel(x, norm1, w_in, sb_gain, dn_conv, dn_a_log, dn_dt_bias, dn_gain, hg_lb_logits, hg_gain,
           w_out, norm2, w_gate, w_up, w_down, final_norm):
    b, s, d = x.shape
    depth = w_in.shape[0]
    lbs = jnp.cumsum(jax.nn.softmax(hg_lb_logits.astype(F32), axis=0), axis=0)
    lbs = lbs - lbs[0:1]
    xf = x.reshape(b * s, d)
    for l in range(depth):
        w_sb, w_rest = _split_w_in(w_in[l])
        sb, rest = _in_proj(xf, norm1[l], w_sb, w_rest, IN_PROJ_ROWS)
        sb_o = _sb_attention(sb.reshape(b, s, -1), sb_gain[l], SB_QUERY_ROWS)
        rest3 = rest.reshape(b, s, -1)
        dn_o = _gdn(rest3, dn_conv[l], dn_a_log[l], dn_dt_bias[l], dn_gain[l], DN_BLOCK_ROWS)
        hg_o = _hgrn2(rest3, lbs[l], hg_gain[l], HG_BLOCK_ROWS)
        xf = _out_ffn(xf, sb_o.reshape(b * s, -1), dn_o.reshape(b * s, -1), hg_o.reshape(b * s, -1),
                      w_out[l].astype(BF16), norm2[l], w_gate[l].astype(BF16), w_up[l].astype(BF16),
                      w_down[l].astype(BF16), final_norm, l == depth - 1, FFN_ROWS)
    return xf.reshape(b, s, d)
```

```python
import functools

import jax
import jax.numpy as jnp
from jax import lax
from jax.experimental import pallas as pl
from jax.experimental.pallas import tpu as pltpu

EPS = 1e-6
F32 = jnp.float32
BF16 = jnp.bfloat16
LOG2E = 1.4426950408889634

SB_HEAD_DIM = 64
SB_WIDTH = 256
SB_UNROLL = 8
DN_HEAD_DIM = 128
DN_HEADS = 4
DN_WIDTH = 512
DN_CONV = 4
DN_CHUNK = 64
DN_SUB = 16
DN_PREP_GROUP = 8
HG_HEADS = 4
HG_DIM = 64
HG_WIDTH = 256
HG_CHUNK = 16
HG_GROUP = 4
LANES = 128
SUBLANES = 8
VMEM_LIMIT_BYTES = 56 * 1024 * 1024

IN_PROJ_ROWS = 512
SB_QUERY_ROWS = 1024
DN_BLOCK_ROWS = 512
HG_BLOCK_ROWS = 256
FFN_ROWS = 512

_DONE = object()


def _params(*sem):
    return pltpu.CompilerParams(dimension_semantics=sem, vmem_limit_bytes=VMEM_LIMIT_BYTES)


def _split3(a):
    hi = a.astype(BF16)
    lo = (a - hi.astype(F32)).astype(BF16)
    return hi, lo


def _dot(a, b):
    return jnp.dot(a, b, preferred_element_type=F32)


def _dot_nt(a, b):
    return lax.dot_general(a, b, (((1,), (1,)), ((), ())), preferred_element_type=F32)


def _dot_tn(a, b):
    return lax.dot_general(a, b, (((0,), (0,)), ((), ())), preferred_element_type=F32)


def _dot_exact_rhs(a, b_bf16):
    ah, al = _split3(a)
    return _dot(ah, b_bf16) + _dot(al, b_bf16)


def _dot_exact_lhs(a_bf16, b):
    bh, bl = _split3(b)
    return _dot(a_bf16, bh) + _dot(a_bf16, bl)


def _softplus(x):
    neg_abs = lax.bitcast_convert_type(
        lax.bitcast_convert_type(x, jnp.uint32) | jnp.uint32(0x80000000), F32)
    return jnp.maximum(x, 0.0) + jnp.log(1.0 + jnp.exp2(neg_abs * LOG2E))


def _softplus2(x2):
    neg_abs = lax.bitcast_convert_type(
        lax.bitcast_convert_type(x2, jnp.uint32) | jnp.uint32(0x80000000), F32)
    return jnp.maximum(x2, 0.0) + jnp.log(1.0 + jnp.exp2(neg_abs)) * LOG2E


def _sigmoid(x):
    return 1.0 / (1.0 + jnp.exp(-x))


def _in_proj_kernel(x_ref, g_ref, wsb_ref, wrest_ref, sbscale_ref, sb_ref, rest_ref):
    x = x_ref[...]
    y = (x * lax.rsqrt(jnp.mean(x * x, axis=-1, keepdims=True) + EPS) * g_ref[...]).astype(BF16)
    sb_ref[...] = (_dot(y, wsb_ref[...]) * sbscale_ref[...]).astype(sb_ref.dtype)
    rest_ref[...] = _dot(y, wrest_ref[...])


def _in_proj(x, gain, w_sb, w_rest, tm):
    m, d = x.shape
    nsb, nrest = w_sb.shape[1], w_rest.shape[1]
    sbscale = jnp.concatenate([jnp.full((1, SB_WIDTH), SB_HEAD_DIM ** -0.5 * LOG2E, F32),
                               jnp.ones((1, nsb - SB_WIDTH), F32)], axis=1)
    whole = lambda a: pl.BlockSpec(a.shape, lambda i: (0,) * a.ndim, pipeline_mode=pl.Buffered(1))
    gain = gain.astype(F32).reshape(1, d)
    return pl.pallas_call(
        _in_proj_kernel,
        grid=(m // tm,),
        in_specs=[pl.BlockSpec((tm, d), lambda i: (i, 0)), whole(gain), whole(w_sb), whole(w_rest),
                  whole(sbscale)],
        out_specs=[pl.BlockSpec((tm, nsb), lambda i: (i, 0)), pl.BlockSpec((tm, nrest), lambda i: (i, 0))],
        out_shape=[jax.ShapeDtypeStruct((m, nsb), BF16), jax.ShapeDtypeStruct((m, nrest), F32)],
        compiler_params=_params("parallel"),
        name="in_proj",
    )(x, gain, w_sb, w_rest, sbscale)


def _sb_attn_kernel(q_ref, k_ref, v_ref, gain_ref, o_ref, acc_ref, off0_ref, off1_ref,
                    kx_ref, vx_ref, *, tq):
    tk = LANES
    ndiag = tq // tk
    qi = pl.program_id(2)
    q = q_ref[0]

    @pl.when(qi == 0)
    def _():
        head0 = lax.broadcasted_iota(jnp.int32, (tk, LANES), 1) < SB_HEAD_DIM

        def expand(j, carry):
            ks = pl.multiple_of(j * tk, tk)
            for src, dst in ((k_ref, kx_ref), (v_ref, vx_ref)):
                x = src[0, pl.ds(ks, tk), :]
                zero = jnp.zeros_like(x)
                dst[j] = jnp.concatenate([jnp.where(head0, x, zero), jnp.where(head0, zero, x)], axis=0)
            return carry

        lax.fori_loop(0, kx_ref.shape[0], expand, 0)

    row = lax.broadcasted_iota(jnp.int32, (2 * tk, 2 * tk), 0)
    col = lax.broadcasted_iota(jnp.int32, (2 * tk, 2 * tk), 1)
    rev = jnp.where((row >= col) & ((row >= tk) == (col >= tk)), 1.0, 0.0).astype(BF16)
    qrow = lax.broadcasted_iota(jnp.int32, (tq, 2 * tk), 0)
    kcol = lax.broadcasted_iota(jnp.int32, (tq, 2 * tk), 1)
    kcol = jnp.where(kcol >= tk, kcol - tk, kcol)

    acc_ref[...] = jnp.zeros_like(acc_ref)
    off0_ref[...] = jnp.zeros_like(off0_ref)
    off1_ref[...] = jnp.zeros_like(off1_ref)

    def scores(kj, past, r0):
        z = _dot_nt(q[r0:], kx_ref[kj])
        sp = _softplus2(z)
        if past is not None:
            sp = jnp.where(past[r0:], sp, 0.0)
        tot0 = jnp.broadcast_to(jnp.sum(sp[:, :tk], axis=1, keepdims=True), (tq - r0, LANES))
        tot1 = jnp.broadcast_to(jnp.sum(sp[:, tk:], axis=1, keepdims=True), (tq - r0, LANES))
        return z, sp.astype(BF16), tot0, tot1

    def accumulate(kj, past, r0, staged, off):
        z, spb, tot0, tot1 = staged
        cum = _dot(spb, rev)
        w = jnp.exp2(z - cum - jnp.concatenate([off[0][r0:], off[1][r0:]], axis=1))
        if past is not None:
            w = jnp.where(past[r0:], w, 0.0)
        acc_ref[r0:, :] += _dot(w.astype(BF16), vx_ref[kj])
        if r0 == 0:
            return off[0] + tot0, off[1] + tot1
        return (jnp.concatenate([off[0][:r0], off[0][r0:] + tot0], axis=0),
                jnp.concatenate([off[1][:r0], off[1][r0:] + tot1], axis=0))

    def walk(tiles):
        off = (off0_ref[...], off1_ref[...])
        staged = scores(*tiles[0])
        for tile, nxt in zip(tiles, tiles[1:] + [None]):
            ahead = scores(*nxt) if nxt is not None else None
            off = accumulate(*tile, staged, off)
            staged = ahead
        off0_ref[...], off1_ref[...] = off

    walk([(qi * ndiag + d, kcol + d * tk < qrow, d * tk) for d in range(ndiag - 1, -1, -1)])

    def body(it, carry):
        kj = qi * ndiag - 1 - SB_UNROLL * it
        walk([(kj - u, None, 0) for u in range(SB_UNROLL)])
        return carry

    lax.fori_loop(0, qi * (ndiag // SB_UNROLL), body, 0)

    head0q = lax.broadcasted_iota(jnp.int32, (tq, LANES), 1) < SB_HEAD_DIM
    o = acc_ref[...]
    oo = o * o
    ms0 = jnp.sum(jnp.where(head0q, oo, 0.0), axis=1, keepdims=True)
    ms1 = jnp.sum(jnp.where(head0q, 0.0, oo), axis=1, keepdims=True)
    ms = jnp.where(head0q, ms0, ms1) * (1.0 / SB_HEAD_DIM)
    o_ref[0] = (o * lax.rsqrt(ms + EPS) * gain_ref[...]).astype(o_ref.dtype)


def _sb_attention(sb, gain, tq):
    b, s, _ = sb.shape
    npair = SB_WIDTH // LANES
    gain2 = jnp.tile(gain.astype(F32), LANES // SB_HEAD_DIM).reshape(1, LANES)
    return pl.pallas_call(
        functools.partial(_sb_attn_kernel, tq=tq),
        grid=(b, npair, s // tq),
        in_specs=[
            pl.BlockSpec((1, tq, LANES), lambda bi, p, qi: (bi, qi, p)),
            pl.BlockSpec((1, s, LANES), lambda bi, p, qi: (bi, 0, npair + p)),
            pl.BlockSpec((1, s, LANES), lambda bi, p, qi: (bi, 0, 2 * npair + p)),
            pl.BlockSpec((1, LANES), lambda bi, p, qi: (0, 0)),
        ],
        out_specs=pl.BlockSpec((1, tq, LANES), lambda bi, p, qi: (bi, qi, p)),
        out_shape=jax.ShapeDtypeStruct((b, s, SB_WIDTH), BF16),
        scratch_shapes=[pltpu.VMEM((tq, LANES), F32)] * 3
        + [pltpu.VMEM((s // LANES, 2 * LANES, LANES), BF16)] * 2,
        compiler_params=_params("parallel", "parallel", "arbitrary"),
        name="sb_attention",
    )(sb, sb, sb, gain2)


def _block_diag(b, same_head):
    tiled = jnp.concatenate([b] * DN_HEADS, axis=0)
    return jnp.where(same_head, tiled, jnp.zeros_like(tiled))


def _hp_dot1(a, b, same_head):
    return _dot(a.astype(BF16), _block_diag(b.astype(BF16), same_head))


def _hp_dot3(a, b, same_head):
    ah, al = _split3(a)
    bh, bl = _split3(b)
    wh = _block_diag(bh, same_head)
    return _dot(ah, wh) + (_dot(ah, _block_diag(bl, same_head)) + _dot(al, wh))


def _hp_unit_lower_inverse(m, eye, same_sub, same_head):
    d = jnp.where(same_sub, m, 0.0)
    low = m - d
    p = eye - d
    dk = d
    for _ in range(DN_SUB.bit_length() - 2):
        dk = _hp_dot1(dk, dk, same_head)
        yield
        p = p + _hp_dot1(p, dk, same_head)
    yield
    n = _hp_dot1(p, low, same_head)
    yield
    t = eye - n
    nk = n
    for _ in range((DN_CHUNK // DN_SUB).bit_length() - 2):
        nk = _hp_dot1(nk, nk, same_head)
        yield
        t = t + _hp_dot1(t, nk, same_head)
    yield
    x0 = _hp_dot1(t, p, same_head)
    yield
    resid = eye - x0 - _hp_dot3(m, x0, same_head)
    yield
    return x0 + _hp_dot1(x0, resid, same_head)


def _gdn_kernel(qkv_ref, z_ref, ab_ref, cw_ref, alog_ref, dtb_ref, gain_ref, o_ref,
                xpad_ref, y_ref, u_ref, w_ref, qd_ref, kd_ref, attn_ref, cd_ref, state_ref, *, nc):
    nb = qkv_ref.shape[0]
    c = DN_CHUNK
    hc = DN_HEADS * c
    tb = nc * c
    pad = SUBLANES

    @pl.when(pl.program_id(0) == 0)
    def _():
        state_ref[...] = jnp.zeros_like(state_ref)
        for bi in range(nb):
            xpad_ref[bi, 0:pad, :] = jnp.zeros((pad, xpad_ref.shape[2]), F32)

    for bi in range(nb):
        xpad_ref[bi, pad:pad + tb, :] = qkv_ref[bi]
        conv = xpad_ref[bi, pl.ds(pad - DN_CONV + 1, tb), :] * cw_ref[0:1, :]
        for kk in range(1, DN_CONV):
            conv = conv + xpad_ref[bi, pl.ds(pad - DN_CONV + 1 + kk, tb), :] * cw_ref[kk:kk + 1, :]
        xpad_ref[bi, 0:pad, :] = xpad_ref[bi, tb:tb + pad, :]
        y_ref[bi * tb:(bi + 1) * tb, :] = conv * _sigmoid(conv)

    prow = lax.broadcasted_iota(jnp.int32, (c, hc), 0)
    pcol = lax.broadcasted_iota(jnp.int32, (c, hc), 1) % c
    causal = prow >= pcol
    strict = prow > pcol
    eye = jnp.where(prow == pcol, 1.0, 0.0).astype(F32)
    same_sub = (prow // DN_SUB) == (pcol // DN_SUB)
    brow = lax.broadcasted_iota(jnp.int32, (hc, hc), 0) // c
    bcol = lax.broadcasted_iota(jnp.int32, (hc, hc), 1) // c
    same_head = brow == bcol
    krow = lax.broadcasted_iota(jnp.int32, (hc, DN_WIDTH), 0) // c
    kcol = lax.broadcasted_iota(jnp.int32, (hc, DN_WIDTH), 1) // DN_HEAD_DIM
    same_head_k = krow == kcol
    r16 = lax.broadcasted_iota(jnp.int32, (c, c), 0)
    c16 = lax.broadcasted_iota(jnp.int32, (c, c), 1)
    tril = jnp.where(r16 >= c16, 1.0, 0.0).astype(BF16)
    ones_cc = jnp.ones((c, c), BF16)
    sl = lax.broadcasted_iota(jnp.int32, (LANES, DN_WIDTH), 0)
    sh = lax.broadcasted_iota(jnp.int32, (LANES, DN_WIDTH), 1) // DN_HEAD_DIM
    sel_g_wide = jnp.where(sl == sh, 1.0, 0.0).astype(BF16)
    sel_b_wide = jnp.where(sl == sh + DN_HEADS, 1.0, 0.0).astype(BF16)
    pl_ = lax.broadcasted_iota(jnp.int32, (LANES, hc), 0)
    ph = lax.broadcasted_iota(jnp.int32, (LANES, hc), 1) // c
    sel_g_packed = jnp.where(pl_ == ph, 1.0, 0.0).astype(BF16)
    alog = alog_ref[...]
    dtb = dtb_ref[...]
    gain = gain_ref[...]

    def per_head(x, fn):
        return jnp.concatenate(
            [fn(x[:, h * DN_HEAD_DIM:(h + 1) * DN_HEAD_DIM]) for h in range(DN_HEADS)], axis=1)

    def l2_scale(x):
        inv = lax.rsqrt(jnp.sum(x * x, axis=-1, keepdims=True) + EPS)
        return jnp.broadcast_to(inv, x.shape)

    def prepare(ci, bi):
        ab = ab_ref[bi, pl.ds(pl.multiple_of(ci * c, c), c), :]
        r0 = pl.multiple_of(bi * tb + ci * c, c)
        s0 = pl.multiple_of((bi * nc + ci) * hc, hc)
        g_all = -jnp.exp(alog) * _softplus(ab + dtb)
        beta_all = _sigmoid(ab)
        gc_all = _dot_exact_lhs(tril, g_all)
        cd_ref[pl.ds(r0, c), :] = jnp.exp(gc_all)
        gcol_w = _dot_exact_rhs(gc_all, sel_g_wide)
        beta_w = _dot_exact_rhs(beta_all, sel_b_wide)
        gcol = _dot_exact_rhs(gc_all, sel_g_packed)
        grow = _dot_exact_lhs(ones_cc, gcol * eye)
        decay = jnp.where(causal, jnp.exp(jnp.where(causal, gcol - grow, 0.0)), 0.0)

        q = y_ref[pl.ds(r0, c), 0:DN_WIDTH]
        k = y_ref[pl.ds(r0, c), DN_WIDTH:2 * DN_WIDTH]
        v = y_ref[pl.ds(r0, c), 2 * DN_WIDTH:3 * DN_WIDTH]
        q = q * per_head(q, l2_scale) * (DN_HEAD_DIM ** -0.5)
        k = k * per_head(k, l2_scale)
        kb = k * beta_w
        k_bd = _block_diag_k(k.astype(BF16), same_head_k)
        kk_t = _dot_nt(kb.astype(BF16), k_bd)
        qk_t = _dot_nt(q.astype(BF16), k_bd)
        yield
        t_inv = yield from _hp_unit_lower_inverse(
            jnp.where(strict, kk_t * decay, 0.0), eye, same_sub, same_head)
        yield

        egc_w = jnp.exp(gcol_w)
        vb = v * beta_w
        kbe = kb * egc_w
        rhs = jnp.concatenate(
            [jnp.concatenate([vb[:, h * DN_HEAD_DIM:(h + 1) * DN_HEAD_DIM],
                              kbe[:, h * DN_HEAD_DIM:(h + 1) * DN_HEAD_DIM]], axis=1)
             for h in range(DN_HEADS)], axis=0)
        th, tl = _split3(_block_diag(t_inv, same_head))
        rh, rl = _split3(rhs)
        uw = _dot(th, rh) + (_dot(th, rl) + _dot(tl, rh))
        u_ref[pl.ds(s0, hc), :] = uw[:, :DN_HEAD_DIM]
        w_ref[pl.ds(s0, hc), :] = uw[:, DN_HEAD_DIM:].astype(BF16)
        attn_ref[pl.ds(s0, hc), :] = _block_diag((qk_t * decay).astype(BF16), same_head)
        qd_ref[pl.ds(r0, c), :] = (q * egc_w).astype(BF16)
        kd_ref[pl.ds(r0, c), :] = (k * jnp.exp(gcol_w[c - 1:c, :] - gcol_w)).astype(BF16)

    def run_interleaved(gens):
        while gens:
            gens = [g for g in gens if next(g, _DONE) is not _DONE]

    for bi in range(nb):
        def prepare_group(gi, carry, bi=bi):
            run_interleaved([prepare(gi * DN_PREP_GROUP + j, bi) for j in range(DN_PREP_GROUP)])
            return carry

        lax.fori_loop(0, nc // DN_PREP_GROUP, prepare_group, 0)

    def scan(ci, bi):
        t0 = pl.multiple_of(ci * c, c)
        r0 = pl.multiple_of(bi * tb + ci * c, c)
        s0 = pl.multiple_of((bi * nc + ci) * hc, hc)
        cd_all = cd_ref[pl.ds(r0 + c - 1, 1), :]
        states = [state_ref[bi * DN_HEADS + h] for h in range(DN_HEADS)]
        sbf = [s.astype(BF16) for s in states]
        v_new = [u_ref[pl.ds(s0 + h * c, c), :] - _dot(w_ref[pl.ds(s0 + h * c, c), :], sbf[h])
                 for h in range(DN_HEADS)]
        yield
        vnb = [x.astype(BF16) for x in v_new]
        o_intra = _dot(attn_ref[pl.ds(s0, hc), :], jnp.concatenate(vnb, axis=0))
        for h in range(DN_HEADS):
            hs = slice(h * DN_HEAD_DIM, (h + 1) * DN_HEAD_DIM)
            o = _dot(qd_ref[pl.ds(r0, c), hs], sbf[h]) + o_intra[h * c:(h + 1) * c, :]
            state_ref[bi * DN_HEADS + h] = states[h] * cd_all[:, h:h + 1] + _dot_tn(
                kd_ref[pl.ds(r0, c), hs], vnb[h])
            zz = z_ref[bi, pl.ds(t0, c), hs]
            on = o * lax.rsqrt(jnp.mean(o * o, axis=-1, keepdims=True) + EPS) * gain
            o_ref[bi, pl.ds(t0, c), hs] = (on * (zz * _sigmoid(zz))).astype(o_ref.dtype)

    def scan_step(ci, carry):
        run_interleaved([scan(ci, bi) for bi in range(nb)])
        return carry

    lax.fori_loop(0, nc, scan_step, 0)


def _block_diag_k(k_bf16, same_head_k):
    tiled = jnp.concatenate([k_bf16] * DN_HEADS, axis=0)
    return jnp.where(same_head_k, tiled, jnp.zeros_like(tiled))


def _gdn(rest, conv_w, a_log, dt_bias, gain, tb):
    b, s, _ = rest.shape
    nc = tb // DN_CHUNK
    wq = 3 * DN_WIDTH
    hrows = DN_HEADS * tb

    def lanes(vec, offset):
        return jnp.zeros((1, LANES), F32).at[0, offset:offset + vec.shape[0]].set(vec.astype(F32))

    return pl.pallas_call(
        functools.partial(_gdn_kernel, nc=nc),
        grid=(s // tb,),
        in_specs=[
            pl.BlockSpec((b, tb, wq), lambda t: (0, t, 0)),
            pl.BlockSpec((b, tb, DN_WIDTH), lambda t: (0, t, wq // DN_WIDTH)),
            pl.BlockSpec((b, tb, LANES), lambda t: (0, t, _AB_COL // LANES)),
            pl.BlockSpec((DN_CONV, wq), lambda t: (0, 0)),
            pl.BlockSpec((1, LANES), lambda t: (0, 0)),
            pl.BlockSpec((1, LANES), lambda t: (0, 0)),
            pl.BlockSpec((1, DN_HEAD_DIM), lambda t: (0, 0)),
        ],
        out_specs=pl.BlockSpec((b, tb, DN_WIDTH), lambda t: (0, t, 0)),
        out_shape=jax.ShapeDtypeStruct((b, s, DN_WIDTH), BF16),
        scratch_shapes=[
            pltpu.VMEM((b, tb + 2 * SUBLANES, wq), F32),
            pltpu.VMEM((b * tb, wq), F32),
            pltpu.VMEM((b * hrows, DN_HEAD_DIM), F32),
            pltpu.VMEM((b * hrows, DN_HEAD_DIM), BF16),
            pltpu.VMEM((b * tb, DN_WIDTH), BF16),
            pltpu.VMEM((b * tb, DN_WIDTH), BF16),
            pltpu.VMEM((b * hrows, DN_HEADS * DN_CHUNK), BF16),
            pltpu.VMEM((b * tb, LANES), F32),
            pltpu.VMEM((b * DN_HEADS, DN_HEAD_DIM, DN_HEAD_DIM), F32),
        ],
        compiler_params=_params("arbitrary"),
        name="gated_deltanet",
    )(rest, rest, rest, conv_w.astype(F32), lanes(a_log, 0), lanes(dt_bias, 0),
      gain.astype(F32).reshape(1, DN_HEAD_DIM))


def _hgrn2_kernel(q_ref, f_ref, i_ref, g_ref, lb_ref, gain_ref, o_ref, acc_ref, state_ref, *, nsub):
    nb = q_ref.shape[0]
    tb = nsub * HG_CHUNK
    c = HG_CHUNK
    w = HG_WIDTH

    @pl.when(pl.program_id(0) == 0)
    def _():
        state_ref[...] = jnp.zeros_like(state_ref)

    row = lax.broadcasted_iota(jnp.int32, (w, w), 0)
    col = lax.broadcasted_iota(jnp.int32, (w, w), 1)
    same_head = (row // HG_DIM) == (col // HG_DIM)
    head_ones = jnp.where(same_head, 1.0, 0.0).astype(BF16)
    r16 = lax.broadcasted_iota(jnp.int32, (c, c), 0)
    c16 = lax.broadcasted_iota(jnp.int32, (c, c), 1)
    tril = jnp.where(r16 >= c16, 1.0, 0.0).astype(BF16)
    trow = lax.broadcasted_iota(jnp.int32, (c, w), 0)
    lb = lb_ref[...]
    log_lb = jnp.log(lb)
    log_1mlb = jnp.log(1.0 - lb)

    def sub(si, bi, run):
        r0 = pl.multiple_of(si * c, c)
        hq = q_ref[bi, pl.ds(r0, c), :]
        hf = f_ref[bi, pl.ds(r0, c), :]
        v = i_ref[bi, pl.ds(r0, c), :]
        q = hq * _sigmoid(hq)
        log_sig = jnp.minimum(hf, 0.0) - jnp.log(1.0 + jnp.exp(-jnp.abs(hf)))
        bterm = log_1mlb + log_sig
        mx = jnp.maximum(log_lb, bterm)
        lf = mx + jnp.log(jnp.exp(log_lb - mx) + jnp.exp(bterm - mx))
        k = (1.0 - lb) * _sigmoid(-hf)
        bcum = _dot_exact_lhs(tril, lf)
        last = bcum[c - 1:c, :]
        q_dec = (q * jnp.exp(bcum)).astype(BF16)
        xs = []
        for s_i in range(c):
            keep = trow >= s_i
            dec = jnp.where(keep, jnp.exp(jnp.where(keep, bcum - bcum[s_i:s_i + 1, :], 0.0)), 0.0)
            xs.append((dec * q * k[s_i:s_i + 1, :]).astype(BF16))
        scores = _dot(jnp.concatenate(xs, axis=0), head_ones)
        intra = scores[0:c, :] * v[0:1, :]
        for s_i in range(1, c):
            intra = intra + scores[s_i * c:(s_i + 1) * c, :] * v[s_i:s_i + 1, :]
        kd = k * jnp.exp(last - bcum)
        upd = _dot_tn(v.astype(BF16), kd.astype(BF16))
        e_last = jnp.exp(last)
        yield
        st = run[bi]
        acc_ref[pl.ds(pl.multiple_of(bi * tb + si * c, c), c), :] = _dot_nt(q_dec, st.astype(BF16)) + intra
        run[bi] = jnp.where(same_head, st * e_last + upd, 0.0)

    def sub_group(gi, carry):
        run = {bi: state_ref[bi] for bi in range(nb)}
        running = [sub(gi * HG_GROUP + j, bi, run) for j in range(HG_GROUP) for bi in range(nb)]
        while running:
            running = [g for g in running if next(g, _DONE) is not _DONE]
        for bi in range(nb):
            state_ref[bi] = run[bi]
        return carry

    lax.fori_loop(0, nsub // HG_GROUP, sub_group, 0)

    for bi in range(nb):
        o = acc_ref[bi * tb:(bi + 1) * tb, :]
        ms = _dot_exact_rhs(o * o, head_ones) * (1.0 / HG_DIM)
        gate = g_ref[bi]
        o_ref[bi] = (o * lax.rsqrt(ms + EPS) * gain_ref[...] * _sigmoid(gate)).astype(o_ref.dtype)


def _hgrn2(rest, lb, gain, tb):
    b, s, _ = rest.shape
    cb = _HG_COL // HG_WIDTH
    blk = lambda j: pl.BlockSpec((b, tb, HG_WIDTH), lambda t: (0, t, cb + j))
    vec = pl.BlockSpec((1, HG_WIDTH), lambda t: (0, 0))
    return pl.pallas_call(
        functools.partial(_hgrn2_kernel, nsub=tb // HG_CHUNK),
        grid=(s // tb,),
        in_specs=[blk(0), blk(1), blk(2), blk(3), vec, vec],
        out_specs=pl.BlockSpec((b, tb, HG_WIDTH), lambda t: (0, t, 0)),
        out_shape=jax.ShapeDtypeStruct((b, s, HG_WIDTH), BF16),
        scratch_shapes=[
            pltpu.VMEM((b * tb, HG_WIDTH), F32),
            pltpu.VMEM((b, HG_WIDTH, HG_WIDTH), F32),
        ],
        compiler_params=_params("arbitrary"),
        name="hgrn2",
    )(rest, rest, rest, rest, lb.astype(F32).reshape(1, HG_WIDTH),
      jnp.tile(gain.astype(F32), HG_HEADS).reshape(1, HG_WIDTH))


def _rms(x, gain):
    return x * lax.rsqrt(jnp.mean(x * x, axis=-1, keepdims=True) + EPS) * gain


def _out_ffn_kernel(x_ref, sb_ref, dn_ref, hg_ref, wsb_ref, wdn_ref, whg_ref, n2_ref,
                    wg_ref, wu_ref, wd_ref, fn_ref, o_ref, *, final):
    x1 = x_ref[...] + _dot(sb_ref[...], wsb_ref[...]) + _dot(dn_ref[...], wdn_ref[...]) \
        + _dot(hg_ref[...], whg_ref[...])
    h2 = _rms(x1, n2_ref[...]).astype(BF16)
    g = _dot(h2, wg_ref[...])
    u = _dot(h2, wu_ref[...])
    ff = (g * _sigmoid(g) * u).astype(BF16)
    out = x1 + _dot(ff, wd_ref[...])
    if final:
        out = _rms(out, fn_ref[...])
    o_ref[...] = out


def _out_ffn(x, sb_o, dn_o, hg_o, w_out, norm2, w_gate, w_up, w_down, final_norm, final, tm):
    m, d = x.shape
    rows = lambda width: pl.BlockSpec((tm, width), lambda i: (i, 0))
    whole = lambda a: pl.BlockSpec(a.shape, lambda i: (0,) * a.ndim, pipeline_mode=pl.Buffered(1))
    wsb = w_out[:SB_WIDTH]
    wdn = w_out[SB_WIDTH:SB_WIDTH + DN_WIDTH]
    whg = w_out[SB_WIDTH + DN_WIDTH:]
    n2 = norm2.astype(F32).reshape(1, d)
    fn = final_norm.astype(F32).reshape(1, d)
    args = (x, sb_o, dn_o, hg_o, wsb, wdn, whg, n2, w_gate, w_up, w_down, fn)
    in_specs = [rows(d), rows(SB_WIDTH), rows(DN_WIDTH), rows(HG_WIDTH)] + [whole(a) for a in args[4:]]
    return pl.pallas_call(
        functools.partial(_out_ffn_kernel, final=final),
        grid=(m // tm,),
        in_specs=in_specs,
        out_specs=rows(d),
        out_shape=jax.ShapeDtypeStruct((m, d), F32),
        compiler_params=_params("parallel"),
        name="out_ffn",
    )(*args)


_HG_COL = 3 * DN_WIDTH + DN_WIDTH
_AB_COL = _HG_COL + 4 * HG_WIDTH


def _split_w_in(w_in):
    sb = w_in[:, :3 * SB_WIDTH]
    o = 3 * SB_WIDTH
    dn_qkvz = w_in[:, o:o + 4 * DN_WIDTH]
    o += 4 * DN_WIDTH
    dn_ab = w_in[:, o:o + 2 * DN_HEADS]
    o += 2 * DN_HEADS
    hg = w_in[:, o:]
    pad = jnp.zeros((w_in.shape[0], LANES - 2 * DN_HEADS), w_in.dtype)
    rest = jnp.concatenate([dn_qkvz, hg, dn_ab, pad], axis=1)
    return sb.astype(BF16), rest.astype(BF16)


def kernel(x, norm1, w_in, sb_gain, dn_conv, dn_a_log, dn_dt_bias, dn_gain, hg_lb_logits, hg_gain,
           w_out, norm2, w_gate, w_up, w_down, final_norm):
    b, s, d = x.shape
    depth = w_in.shape[0]
    lbs = jnp.cumsum(jax.nn.softmax(hg_lb_logits.astype(F32), axis=0), axis=0)
    lbs = lbs - lbs[0:1]
    xf = x.reshape(b * s, d)
    for l in range(depth):
        w_sb, w_rest = _split_w_in(w_in[l])
        sb, rest = _in_proj(xf, norm1[l], w_sb, w_rest, IN_PROJ_ROWS)
        sb_o = _sb_attention(sb.reshape(b, s, -1), sb_gain[l], SB_QUERY_ROWS)
        rest3 = rest.reshape(b, s, -1)
        dn_o = _gdn(rest3, dn_conv[l], dn_a_log[l], dn_dt_bias[l], dn_gain[l], DN_BLOCK_ROWS)
        hg_o = _hgrn2(rest3, lbs[l], hg_gain[l], HG_BLOCK_ROWS)
        xf = _out_ffn(xf, sb_o.reshape(b * s, -1), dn_o.reshape(b * s, -1), hg_o.reshape(b * s, -1),
                      w_out[l].astype(BF16), norm2[l], w_gate[l].astype(BF16), w_up[l].astype(BF16),
                      w_down[l].astype(BF16), final_norm, l == depth - 1, FFN_ROWS)
    return xf.reshape(b, s, d)
```
